```python
import math
import jax
import jax.numpy as jnp
from jax import lax
import numpy as np

D_MODEL = 4096
BATCH = 2
SEQ = 8192
DEPTH = 2

PLE_DIM = 256
HEAD_DIM = 128
D_ATTN = D_MODEL // 2
N_ATTN_HEADS = D_ATTN // HEAD_DIM
D_CONV = D_MODEL // 2
CONV_WIDTH = 31
MOBA_BLOCK = 256
MOBA_TOP_K = 3
QUERY_CHUNK = 32
LN_EPS = 1e-5
DEEPNORM_ALPHA = (2.0 * DEPTH) ** 0.25
DEEPNORM_BETA = (8.0 * DEPTH) ** -0.25

IN_WIDTHS = (D_ATTN, D_ATTN, D_ATTN, D_ATTN, D_CONV, D_CONV, D_CONV, D_MODEL, D_MODEL)
IN_BETA_SCALED = (False, False, True, False, True, False, False, False, False)
D_IN = sum(IN_WIDTHS)
SPLIT_POINTS = tuple(int(s) for s in np.cumsum(IN_WIDTHS)[:-1])

kernel_name = "hybrid_moba_conformer_gated_deepnorm"


def layer_norm(x, g, b):
    xf = x.astype(jnp.float32)
    mu = jnp.mean(xf, axis=-1, keepdims=True)
    var = jnp.mean(jnp.square(xf - mu), axis=-1, keepdims=True)
    return ((xf - mu) * lax.rsqrt(var + LN_EPS) * g.astype(jnp.float32)
            + b.astype(jnp.float32)).astype(x.dtype)


def moba_attention(q, k, v):
    B, T, H, Dh = q.shape
    n_blk = -(-T // MOBA_BLOCK)
    t_pad = n_blk * MOBA_BLOCK
    top_k = min(MOBA_TOP_K, n_blk)
    pad = ((0, 0), (0, t_pad - T), (0, 0), (0, 0))
    q, k, v = (jnp.pad(t, pad).transpose(0, 2, 1, 3) for t in (q, k, v))
    kb = k.reshape(B, H, n_blk, MOBA_BLOCK, Dh)
    vb = v.reshape(B, H, n_blk, MOBA_BLOCK, Dh)
    k_mean = jnp.mean(kb.astype(jnp.float32), axis=3).astype(k.dtype)
    n_chunks = t_pad // QUERY_CHUNK
    chunks_per_block = MOBA_BLOCK // QUERY_CHUNK
    q_chunks = q.reshape(B, H, n_chunks, QUERY_CHUNK, Dh).transpose(2, 0, 1, 3, 4)
    scale = Dh ** -0.5
    bi = jnp.arange(B)[:, None, None, None]
    hi = jnp.arange(H)[None, :, None, None]
    blk_ids = jnp.arange(n_blk)
    in_blk = jnp.arange(MOBA_BLOCK)

    def chunk_attend(args):
        c, q_c = args
        own = c // chunks_per_block
        q_pos = c * QUERY_CHUNK + jnp.arange(QUERY_CHUNK)
        gate = jnp.einsum('bhqd,bhnd->bhqn', q_c, k_mean).astype(jnp.float32)
        gate = jnp.where((blk_ids < own)[None, None, None, :], gate, -jnp.inf)
        _, sel = lax.top_k(gate, top_k)
        sel_valid = sel < own
        k_sel = kb[bi, hi, sel]
        v_sel = vb[bi, hi, sel]
        s_sel = jnp.einsum('bhqd,bhqnkd->bhqnk', q_c, k_sel).astype(jnp.float32) * scale
        s_sel = jnp.where(sel_valid[..., None], s_sel, -jnp.inf)
        s_sel = s_sel.reshape(B, H, QUERY_CHUNK, top_k * MOBA_BLOCK)
        k_own = lax.dynamic_index_in_dim(kb, own, axis=2, keepdims=False)
        v_own = lax.dynamic_index_in_dim(vb, own, axis=2, keepdims=False)
        s_own = jnp.einsum('bhqd,bhkd->bhqk', q_c, k_own).astype(jnp.float32) * scale
        k_pos = own * MOBA_BLOCK + in_blk
        s_own = jnp.where((k_pos[None, :] <= q_pos[:, None])[None, None], s_own, -jnp.inf)
        probs = jax.nn.softmax(jnp.concatenate([s_sel, s_own], axis=-1), axis=-1).astype(v.dtype)
        p_sel = probs[..., :top_k * MOBA_BLOCK].reshape(B, H, QUERY_CHUNK, top_k, MOBA_BLOCK)
        p_own = probs[..., top_k * MOBA_BLOCK:]
        return (jnp.einsum('bhqnk,bhqnkd->bhqd', p_sel, v_sel)
                + jnp.einsum('bhqk,bhkd->bhqd', p_own, v_own))

    out = lax.map(chunk_attend, (jnp.arange(n_chunks), q_chunks))
    out = out.transpose(1, 0, 3, 2, 4).reshape(B, t_pad, H * Dh)
    return out[:, :T]


def causal_depthwise_conv(u, w, b):
    out = lax.conv_general_dilated(
        u, w[:, None, :], window_strides=(1,), padding=[(CONV_WIDTH - 1, 0)],
        dimension_numbers=('NWC', 'WIO', 'NWC'), feature_group_count=u.shape[-1])
    return out + b


def hybrid_layer(x, p_i, w_in, b_in, w_conv, b_conv, conv_ln_g, conv_ln_b,
                 w_o_attn, w_o_conv, w_out, w_ple_up, w_ple_gate, b_ple_gate, ln_g, ln_b):
    B, T, _ = x.shape
    h = jnp.einsum('btd,de->bte', x, w_in) + b_in
    q, k, v, z_a, c_val, c_gate, z_c, g_a, g_c = jnp.split(h, SPLIT_POINTS, axis=-1)
    shp = (B, T, N_ATTN_HEADS, HEAD_DIM)
    attn = moba_attention(q.reshape(shp), k.reshape(shp), v.reshape(shp))
    o_a = jnp.einsum('btc,cd->btd', attn * jax.nn.silu(z_a), w_o_attn)
    u = c_val * jax.nn.sigmoid(c_gate)
    u = causal_depthwise_conv(u, w_conv, b_conv)
    u = jax.nn.silu(layer_norm(u, conv_ln_g, conv_ln_b))
    o_c = jnp.einsum('btc,cd->btd', u * jax.nn.silu(z_c), w_o_conv)
    merged = jax.nn.sigmoid(g_a) * o_a + jax.nn.sigmoid(g_c) * o_c
    y = DEEPNORM_ALPHA * x + jnp.einsum('btd,de->bte', merged, w_out)
    ple = (jax.nn.sigmoid(jnp.einsum('btd,de->bte', y, w_ple_gate) + b_ple_gate)
           * jnp.einsum('btp,pd->btd', p_i, w_ple_up))
    return layer_norm(y + ple, ln_g, ln_b)


def setup_inputs(seed: int = 0) -> dict:
    key = jax.random.key(seed)
    ks = jax.random.split(key, 16)
    f32 = jnp.float32
    nrm = lambda k, shape, s: jax.random.normal(k, shape, f32) * s
    x = nrm(ks[0], (BATCH, SEQ, D_MODEL), 1.0)
    p = nrm(ks[1], (DEPTH, BATCH, SEQ, PLE_DIM), 1.0)
    col_scale = jnp.concatenate([jnp.full((w,), DEEPNORM_BETA if s else 1.0, f32)
                                 for w, s in zip(IN_WIDTHS, IN_BETA_SCALED)])
    w_in = nrm(ks[2], (DEPTH, D_MODEL, D_IN), D_MODEL ** -0.5) * col_scale
    b_in = nrm(ks[3], (DEPTH, D_IN), 0.01)
    w_conv = nrm(ks[4], (DEPTH, CONV_WIDTH, D_CONV), CONV_WIDTH ** -0.5)
    b_conv = nrm(ks[5], (DEPTH, D_CONV), 0.01)
    conv_ln_g = 1.0 + nrm(ks[6], (DEPTH, D_CONV), 0.01)
    conv_ln_b = nrm(ks[7], (DEPTH, D_CONV), 0.01)
    w_o_attn = nrm(ks[8], (DEPTH, D_ATTN, D_MODEL), DEEPNORM_BETA * D_ATTN ** -0.5)
    w_o_conv = nrm(ks[9], (DEPTH, D_CONV, D_MODEL), DEEPNORM_BETA * D_CONV ** -0.5)
    w_out = nrm(ks[10], (DEPTH, D_MODEL, D_MODEL), DEEPNORM_BETA * D_MODEL ** -0.5)
    w_ple_up = nrm(ks[11], (DEPTH, PLE_DIM, D_MODEL), PLE_DIM ** -0.5)
    w_ple_gate = nrm(ks[12], (DEPTH, D_MODEL, D_MODEL), D_MODEL ** -0.5)
    b_ple_gate = nrm(ks[13], (DEPTH, D_MODEL), 0.01)
    ln_g = 1.0 + nrm(ks[14], (DEPTH, D_MODEL), 0.01)
    ln_b = nrm(ks[15], (DEPTH, D_MODEL), 0.01)
    return {"x": x, "p": p, "w_in": w_in, "b_in": b_in, "w_conv": w_conv, "b_conv": b_conv,
            "conv_ln_g": conv_ln_g, "conv_ln_b": conv_ln_b, "w_o_attn": w_o_attn,
            "w_o_conv": w_o_conv, "w_out": w_out, "w_ple_up": w_ple_up,
            "w_ple_gate": w_ple_gate, "b_ple_gate": b_ple_gate, "ln_g": ln_g, "ln_b": ln_b}


def reference(x, p, w_in, b_in, w_conv, b_conv, conv_ln_g, conv_ln_b, w_o_attn, w_o_conv,
              w_out, w_ple_up, w_ple_gate, b_ple_gate, ln_g, ln_b):
    for i in range(DEPTH):
        x = hybrid_layer(x, p[i], w_in[i], b_in[i], w_conv[i], b_conv[i], conv_ln_g[i],
                         conv_ln_b[i], w_o_attn[i], w_o_conv[i], w_out[i], w_ple_up[i],
                         w_ple_gate[i], b_ple_gate[i], ln_g[i], ln_b[i])
    return x
```

```python
import functools
import math

import jax
import jax.numpy as jnp
from jax import lax
from jax.experimental import pallas as pl
from jax.experimental.pallas import tpu as pltpu

HEAD_DIM = 128
MOBA_BLOCK = 256
MOBA_TOP_K = 3
LN_EPS = 1e-5
CONV_HALO = 32
NEG_BIG = -1e30
V7X_VMEM_LIMIT_BYTES = 56 * 1024 * 1024

F32 = jnp.float32
BF16 = jnp.bfloat16


def _compiler_params(n_axes):
    return pltpu.CompilerParams(dimension_semantics=("arbitrary",) * n_axes,
                                vmem_limit_bytes=V7X_VMEM_LIMIT_BYTES)


def _tile(dim, want):
    t = min(dim, want)
    assert dim % t == 0, (dim, want)
    return t


def _sigmoid(x):
    return 1.0 / (1.0 + jnp.exp(-x))


def _in_proj_kernel(x_ref, w_ref, b_ref, o_ref, *, tn, d_attn, d_conv, q_scale):
    col = pl.program_id(0) * tn
    acc = jnp.dot(x_ref[...], w_ref[...], preferred_element_type=F32) + b_ref[...]
    z_a0 = 3 * d_attn
    z_c0 = 4 * d_attn + 2 * d_conv
    gates0 = z_c0 + d_conv
    is_q = col < d_attn
    is_silu = ((col >= z_a0) & (col < z_a0 + d_attn)) | ((col >= z_c0) & (col < gates0))
    is_sig = col >= gates0
    is_plain = jnp.logical_not(is_q | is_silu | is_sig)

    @pl.when(is_q)
    def _():
        o_ref[...] = (acc * q_scale).astype(o_ref.dtype)

    @pl.when(is_plain)
    def _():
        o_ref[...] = acc.astype(o_ref.dtype)

    @pl.when(is_silu)
    def _():
        o_ref[...] = (acc * _sigmoid(acc)).astype(o_ref.dtype)

    @pl.when(is_sig)
    def _():
        o_ref[...] = _sigmoid(acc).astype(o_ref.dtype)


def _in_proj(x16, w16, b, *, d_attn, d_conv):
    m, k = x16.shape
    n = w16.shape[1]
    tm, tn = _tile(m, 1024), _tile(d_attn, 1024)
    assert n % tn == 0 and d_conv % tn == 0
    kern = functools.partial(_in_proj_kernel, tn=tn, d_attn=d_attn, d_conv=d_conv,
                             q_scale=HEAD_DIM ** -0.5)
    return pl.pallas_call(
        kern,
        grid=(n // tn, m // tm),
        in_specs=[pl.BlockSpec((tm, k), lambda j, i: (i, 0)),
                  pl.BlockSpec((k, tn), lambda j, i: (0, j)),
                  pl.BlockSpec((1, tn), lambda j, i: (0, j))],
        out_specs=pl.BlockSpec((tm, tn), lambda j, i: (i, j)),
        out_shape=jax.ShapeDtypeStruct((m, n), BF16),
        compiler_params=_compiler_params(2),
        name="in_proj",
    )(x16, w16, b.reshape(1, n))


def _attn_kernel(q_ref, k_ref, v_ref, za_ref, o_ref, kmean_ref, *, n_blk, blk):
    i = pl.program_id(2)
    n_pad = kmean_ref.shape[0]

    @pl.when(i == 0)
    def _():
        kmean_ref[...] = jnp.zeros_like(kmean_ref)
        for j in range(n_blk):
            kb = k_ref[j * blk:(j + 1) * blk, :].astype(F32)
            kmean_ref[j:j + 1, :] = jnp.sum(kb, axis=0, keepdims=True) * (1.0 / blk)

    q = q_ref[...]
    nt = (((1,), (1,)), ((), ()))
    gate = lax.dot_general(kmean_ref[...].astype(BF16), q, nt, preferred_element_type=F32)
    blk_iota = lax.broadcasted_iota(jnp.int32, gate.shape, 0)
    gate = jnp.where(blk_iota < i, gate, -jnp.inf)
    sel = []
    for _ in range(MOBA_TOP_K):
        mx = jnp.max(gate, axis=0, keepdims=True)
        idx = jnp.min(jnp.where(gate == mx, blk_iota, n_pad), axis=0, keepdims=True)
        sel.append(jnp.where(mx > -jnp.inf, idx, -1))
        gate = jnp.where(blk_iota == idx, -jnp.inf, gate)

    def scores(j):
        start = pl.multiple_of(j * blk, blk)
        kj = k_ref[pl.ds(start, blk), :]
        vj = v_ref[pl.ds(start, blk), :]
        return lax.dot_general(kj, q, nt, preferred_element_type=F32), vj

    def pv_t(vj, p_t):
        tn_dims = (((0,), (0,)), ((), ()))
        return lax.dot_general(vj, p_t.astype(BF16), tn_dims, preferred_element_type=F32)

    def past_block(j, carry):
        m, l, acc = carry
        s_t, vj = scores(j)
        chosen = sel[0] == j
        for s in sel[1:]:
            chosen = chosen | (s == j)
        bmax = jnp.max(s_t, axis=0, keepdims=True)
        m_new = jnp.maximum(m, jnp.where(chosen, bmax, NEG_BIG))
        p_t = jnp.exp(s_t - m_new)
        alpha = jnp.exp(m - m_new)
        l = alpha * l + jnp.where(chosen, jnp.sum(p_t, axis=0, keepdims=True), 0.0)
        acc = alpha * acc + jnp.where(chosen, pv_t(vj, p_t), 0.0)
        return m_new, l, acc

    init = (jnp.full((1, blk), NEG_BIG, F32), jnp.zeros((1, blk), F32),
            jnp.zeros((HEAD_DIM, blk), F32))
    m, l, acc = lax.fori_loop(0, i, past_block, init)

    s_t, vi = scores(i)
    kpos = lax.broadcasted_iota(jnp.int32, s_t.shape, 0)
    qpos = lax.broadcasted_iota(jnp.int32, s_t.shape, 1)
    s_t = jnp.where(kpos <= qpos, s_t, NEG_BIG)
    m_new = jnp.maximum(m, jnp.max(s_t, axis=0, keepdims=True))
    p_t = jnp.exp(s_t - m_new)
    alpha = jnp.exp(m - m_new)
    l = alpha * l + jnp.sum(p_t, axis=0, keepdims=True)
    acc = alpha * acc + pv_t(vi, p_t)

    out = (acc * (1.0 / l)).T
    o_ref[...] = (out * za_ref[...].astype(F32)).astype(o_ref.dtype)


def _attention(h, *, batch, seq, d_attn):
    m = h.shape[0]
    n_heads = d_attn // HEAD_DIM
    blk = MOBA_BLOCK
    assert seq % blk == 0
    n_blk = seq // blk
    n_pad = -(-n_blk // 8) * 8
    kern = functools.partial(_attn_kernel, n_blk=n_blk, blk=blk)
    return pl.pallas_call(
        kern,
        grid=(batch, n_heads, n_blk),
        in_specs=[pl.BlockSpec((blk, HEAD_DIM), lambda b, hd, i: (b * n_blk + i, hd)),
                  pl.BlockSpec((seq, HEAD_DIM), lambda b, hd, i: (b, n_heads + hd)),
                  pl.BlockSpec((seq, HEAD_DIM), lambda b, hd, i: (b, 2 * n_heads + hd)),
                  pl.BlockSpec((blk, HEAD_DIM), lambda b, hd, i: (b * n_blk + i, 3 * n_heads + hd))],
        out_specs=pl.BlockSpec((blk, HEAD_DIM), lambda b, hd, i: (b * n_blk + i, hd)),
        out_shape=jax.ShapeDtypeStruct((m, d_attn), BF16),
        scratch_shapes=[pltpu.VMEM((n_pad, HEAD_DIM), F32)],
        compiler_params=_compiler_params(3),
        name="moba_attention",
    )(h, h, h, h)


def _conv_kernel(cv_ref, cg_ref, hv_ref, hg_ref, zc_ref, w_ref, b_ref, g_ref, beta_ref, o_ref,
                 ext_ref, conv_ref, *, tiles_per_seq, row_chunk, ch_chunk):
    tt, c = cv_ref.shape
    width = w_ref.shape[0]
    first = (pl.program_id(0) % tiles_per_seq) == 0
    halo = hv_ref[...].astype(F32) * _sigmoid(hg_ref[...].astype(F32))
    ext_ref[0:CONV_HALO, :] = jnp.where(first, 0.0, halo)
    ext_ref[CONV_HALO:, :] = cv_ref[...].astype(F32) * _sigmoid(cg_ref[...].astype(F32))
    off = CONV_HALO - (width - 1)
    for r0 in range(0, tt, row_chunk):
        for c0 in range(0, c, ch_chunk):
            acc = jnp.zeros((row_chunk, ch_chunk), F32) + b_ref[:, c0:c0 + ch_chunk]
            for k in range(width):
                acc = acc + (w_ref[k:k + 1, c0:c0 + ch_chunk]
                             * ext_ref[r0 + off + k:r0 + off + k + row_chunk, c0:c0 + ch_chunk])
            conv_ref[r0:r0 + row_chunk, c0:c0 + ch_chunk] = acc
    u = conv_ref[...]
    mu = jnp.mean(u, axis=-1, keepdims=True)
    d = u - mu
    var = jnp.mean(d * d, axis=-1, keepdims=True)
    n = d * lax.rsqrt(var + LN_EPS) * g_ref[...] + beta_ref[...]
    o_ref[...] = (n * _sigmoid(n) * zc_ref[...].astype(F32)).astype(o_ref.dtype)


def _conv_branch(h, w_conv, b_conv, ln_g, ln_b, *, seq, d_attn, d_conv):
    m = h.shape[0]
    c = d_conv
    width = w_conv.shape[0]
    assert width - 1 <= CONV_HALO and (4 * d_attn) % c == 0
    tt = _tile(seq, 256)
    assert tt % CONV_HALO == 0
    cv_blk = (4 * d_attn) // c
    tiles_per_seq = seq // tt
    hpt = tt // CONV_HALO
    kern = functools.partial(_conv_kernel, tiles_per_seq=tiles_per_seq,
                             row_chunk=_tile(tt, 32), ch_chunk=_tile(c, 512))
    row = lambda a: a.reshape(1, c)
    halo_idx = lambda i: jnp.maximum(i * hpt - 1, 0)
    vec_spec = pl.BlockSpec((1, c), lambda i: (0, 0))
    return pl.pallas_call(
        kern,
        grid=(m // tt,),
        in_specs=[pl.BlockSpec((tt, c), lambda i: (i, cv_blk)),
                  pl.BlockSpec((tt, c), lambda i: (i, cv_blk + 1)),
                  pl.BlockSpec((CONV_HALO, c), lambda i: (halo_idx(i), cv_blk)),
                  pl.BlockSpec((CONV_HALO, c), lambda i: (halo_idx(i), cv_blk + 1)),
                  pl.BlockSpec((tt, c), lambda i: (i, cv_blk + 2)),
                  pl.BlockSpec((width, c), lambda i: (0, 0)),
                  vec_spec, vec_spec, vec_spec],
        out_specs=pl.BlockSpec((tt, c), lambda i: (i, 0)),
        out_shape=jax.ShapeDtypeStruct((m, c), BF16),
        scratch_shapes=[pltpu.VMEM((tt + CONV_HALO, c), F32), pltpu.VMEM((tt, c), F32)],
        compiler_params=_compiler_params(1),
        name="conv_branch",
    )(h, h, h, h, h, w_conv, row(b_conv), row(ln_g), row(ln_b))


def _merge_kernel(a_ref, c_ref, woa_ref, woc_ref, ga_ref, gc_ref, o_ref):
    oa = jnp.dot(a_ref[...], woa_ref[...], preferred_element_type=F32)
    oc = jnp.dot(c_ref[...], woc_ref[...], preferred_element_type=F32)
    o_ref[...] = (ga_ref[...].astype(F32) * oa + gc_ref[...].astype(F32) * oc).astype(o_ref.dtype)


def _merge(a, cbr, h, woa16, woc16, *, d_attn, d_conv):
    m = a.shape[0]
    d = woa16.shape[1]
    gates0 = 4 * d_attn + 3 * d_conv
    tm, tn = _tile(m, 1024), _tile(math.gcd(d, gates0), 1024)
    ga_blk, gc_blk = gates0 // tn, (gates0 + d) // tn
    return pl.pallas_call(
        _merge_kernel,
        grid=(d // tn, m // tm),
        in_specs=[pl.BlockSpec((tm, d_attn), lambda j, i: (i, 0)),
                  pl.BlockSpec((tm, d_conv), lambda j, i: (i, 0)),
                  pl.BlockSpec((d_attn, tn), lambda j, i: (0, j)),
                  pl.BlockSpec((d_conv, tn), lambda j, i: (0, j)),
                  pl.BlockSpec((tm, tn), lambda j, i: (i, ga_blk + j)),
                  pl.BlockSpec((tm, tn), lambda j, i: (i, gc_blk + j))],
        out_specs=pl.BlockSpec((tm, tn), lambda j, i: (i, j)),
        out_shape=jax.ShapeDtypeStruct((m, d), BF16),
        compiler_params=_compiler_params(2),
        name="merge",
    )(a, cbr, woa16, woc16, h, h)


def _out_proj_kernel(mg_ref, w_ref, x_ref, y32_ref, y16_ref, *, alpha):
    y = alpha * x_ref[...] + jnp.dot(mg_ref[...], w_ref[...], preferred_element_type=F32)
    y32_ref[...] = y
    y16_ref[...] = y.astype(y16_ref.dtype)


def _out_proj(merged, w16, x32, *, alpha):
    m, d = x32.shape
    tm, tn = _tile(m, 1024), _tile(d, 512)
    tile = pl.BlockSpec((tm, tn), lambda j, i: (i, j))
    return pl.pallas_call(
        functools.partial(_out_proj_kernel, alpha=alpha),
        grid=(d // tn, m // tm),
        in_specs=[pl.BlockSpec((tm, d), lambda j, i: (i, 0)),
                  pl.BlockSpec((d, tn), lambda j, i: (0, j)),
                  tile],
        out_specs=[tile, tile],
        out_shape=[jax.ShapeDtypeStruct((m, d), F32), jax.ShapeDtypeStruct((m, d), BF16)],
        compiler_params=_compiler_params(2),
        name="out_proj",
    )(merged, w16, x32)


def _ple_kernel(y16_ref, wpg_ref, bpg_ref, p_ref, wpu_ref, y32_ref, z_ref):
    gate = jnp.dot(y16_ref[...], wpg_ref[...], preferred_element_type=F32) + bpg_ref[...]
    up = jnp.dot(p_ref[...], wpu_ref[...], preferred_element_type=F32)
    z_ref[...] = y32_ref[...] + _sigmoid(gate) * up


def _ple(y32, y16, p16, wpg16, bpg, wpu16):
    m, d = y32.shape
    pd = p16.shape[1]
    tm, tn = _tile(m, 1024), _tile(d, 512)
    return pl.pallas_call(
        _ple_kernel,
        grid=(d // tn, m // tm),
        in_specs=[pl.BlockSpec((tm, d), lambda j, i: (i, 0)),
                  pl.BlockSpec((d, tn), lambda j, i: (0, j)),
                  pl.BlockSpec((1, tn), lambda j, i: (0, j)),
                  pl.BlockSpec((tm, pd), lambda j, i: (i, 0)),
                  pl.BlockSpec((pd, tn), lambda j, i: (0, j)),
                  pl.BlockSpec((tm, tn), lambda j, i: (i, j))],
        out_specs=pl.BlockSpec((tm, tn), lambda j, i: (i, j)),
        out_shape=jax.ShapeDtypeStruct((m, d), F32),
        compiler_params=_compiler_params(2),
        name="ple",
    )(y16, wpg16, bpg.reshape(1, d), p16, wpu16, y32)


def _ln_kernel(z_ref, g_ref, b_ref, o32_ref, o16_ref):
    z = z_ref[...]
    mu = jnp.mean(z, axis=-1, keepdims=True)
    dz = z - mu
    var = jnp.mean(dz * dz, axis=-1, keepdims=True)
    out = dz * lax.rsqrt(var + LN_EPS) * g_ref[...] + b_ref[...]
    o32_ref[...] = out
    o16_ref[...] = out.astype(o16_ref.dtype)


def _layer_norm(z, g, b):
    m, d = z.shape
    tr = _tile(m, 256)
    tile = pl.BlockSpec((tr, d), lambda i: (i, 0))
    vec = pl.BlockSpec((1, d), lambda i: (0, 0))
    return pl.pallas_call(
        _ln_kernel,
        grid=(m // tr,),
        in_specs=[tile, vec, vec],
        out_specs=[tile, tile],
        out_shape=[jax.ShapeDtypeStruct((m, d), F32), jax.ShapeDtypeStruct((m, d), BF16)],
        compiler_params=_compiler_params(1),
        name="layer_norm",
    )(z, g.reshape(1, d), b.reshape(1, d))


def kernel(x, p, w_in, b_in, w_conv, b_conv, conv_ln_g, conv_ln_b, w_o_attn, w_o_conv, w_out,
           w_ple_up, w_ple_gate, b_ple_gate, ln_g, ln_b):
    batch, seq, d_model = x.shape
    depth = w_in.shape[0]
    d_attn = w_o_attn.shape[1]
    d_conv = w_o_conv.shape[1]
    assert w_in.shape[2] == 4 * d_attn + 3 * d_conv + 2 * d_model
    alpha = (2.0 * depth) ** 0.25
    m = batch * seq
    x32 = x.reshape(m, d_model)
    x16 = x32.astype(BF16)
    for i in range(depth):
        h = _in_proj(x16, w_in[i].astype(BF16), b_in[i], d_attn=d_attn, d_conv=d_conv)
        a = _attention(h, batch=batch, seq=seq, d_attn=d_attn)
        cbr = _conv_branch(h, w_conv[i], b_conv[i], conv_ln_g[i], conv_ln_b[i],
                           seq=seq, d_attn=d_attn, d_conv=d_conv)
        merged = _merge(a, cbr, h, w_o_attn[i].astype(BF16), w_o_conv[i].astype(BF16),
                        d_attn=d_attn, d_conv=d_conv)
        y32, y16 = _out_proj(merged, w_out[i].astype(BF16), x32, alpha=alpha)
        z = _ple(y32, y16, p[i].reshape(m, -1).astype(BF16), w_ple_gate[i].astype(BF16),
                 b_ple_gate[i], w_ple_up[i].astype(BF16))
        x32, x16 = _layer_norm(z, ln_g[i], ln_b[i])
    return x32.reshape(batch, seq, d_model)
```

```python
import functools
import math

import jax
import jax.numpy as jnp
from jax import lax
from jax.experimental import pallas as pl
from jax.experimental.pallas import tpu as pltpu

HEAD_DIM = 128
MOBA_BLOCK = 256
MOBA_TOP_K = 3
LN_EPS = 1e-5
CONV_HALO = 32
NEG_BIG = -1e30
Q_SCALE = HEAD_DIM ** -0.5 * math.log2(math.e)
ATTN_GROUP = 4
ATTN_HEADS_PER_STEP = 2
V7X_VMEM_LIMIT_BYTES = 56 * 1024 * 1024

F32 = jnp.float32
BF16 = jnp.bfloat16


def _compiler_params(n_axes):
    return pltpu.CompilerParams(dimension_semantics=("arbitrary",) * n_axes,
                                vmem_limit_bytes=V7X_VMEM_LIMIT_BYTES)


def _tile(dim, want):
    t = min(dim, want)
    assert dim % t == 0, (dim, want)
    return t


def _sigmoid(x):
    return 1.0 / (1.0 + jnp.exp(-x))


def _in_proj_kernel(x_ref, w_ref, b_ref, o_ref, *, tn, d_attn, d_conv, q_scale):
    col = pl.program_id(0) * tn
    acc = jnp.dot(x_ref[...], w_ref[...], preferred_element_type=F32) + b_ref[...]
    z_a0 = 3 * d_attn
    z_c0 = 4 * d_attn + 2 * d_conv
    gates0 = z_c0 + d_conv
    is_q = col < d_attn
    is_silu = ((col >= z_a0) & (col < z_a0 + d_attn)) | ((col >= z_c0) & (col < gates0))
    is_sig = col >= gates0
    is_plain = jnp.logical_not(is_q | is_silu | is_sig)

    @pl.when(is_q)
    def _():
        o_ref[...] = (acc * q_scale).astype(o_ref.dtype)

    @pl.when(is_plain)
    def _():
        o_ref[...] = acc.astype(o_ref.dtype)

    @pl.when(is_silu)
    def _():
        o_ref[...] = (acc * _sigmoid(acc)).astype(o_ref.dtype)

    @pl.when(is_sig)
    def _():
        o_ref[...] = _sigmoid(acc).astype(o_ref.dtype)


def _in_proj(x16, w16, b, *, d_attn, d_conv):
    m, k = x16.shape
    n = w16.shape[1]
    tm, tn = _tile(m, 1024), _tile(d_attn, 1024)
    assert n % tn == 0 and d_conv % tn == 0
    kern = functools.partial(_in_proj_kernel, tn=tn, d_attn=d_attn, d_conv=d_conv,
                             q_scale=Q_SCALE)
    return pl.pallas_call(
        kern,
        grid=(n // tn, m // tm),
        in_specs=[pl.BlockSpec((tm, k), lambda j, i: (i, 0)),
                  pl.BlockSpec((k, tn), lambda j, i: (0, j)),
                  pl.BlockSpec((1, tn), lambda j, i: (0, j))],
        out_specs=pl.BlockSpec((tm, tn), lambda j, i: (i, j)),
        out_shape=jax.ShapeDtypeStruct((m, n), BF16),
        compiler_params=_compiler_params(2),
        name="in_proj",
    )(x16, w16, b.reshape(1, n))


def _attn_kernel(q_ref, k_ref, v_ref, za_ref, o_ref, kmean_ref, vt_ref, s0_ref, s1_ref, bmax0_ref,
                 bmax1_ref, p_ref, acc_ref, stat_ref, *, n_blk, blk, group):
    i = pl.program_id(2)
    heads = q_ref.shape[1] // HEAD_DIM
    n_pad = kmean_ref.shape[1]
    gk = group * blk
    nt = (((1,), (1,)), ((), ()))
    buf0, buf1 = (s0_ref, bmax0_ref), (s1_ref, bmax1_ref)

    def lanes(hh):
        return slice(hh * HEAD_DIM, (hh + 1) * HEAD_DIM)

    @pl.when(i == 0)
    def _():
        kmean_ref[...] = jnp.zeros_like(kmean_ref)
        vt_ref[:, HEAD_DIM:, :] = jnp.ones((heads, vt_ref.shape[1] - HEAD_DIM, vt_ref.shape[2]), BF16)
        for hh in range(heads):
            for j in range(n_blk):
                rows = slice(j * blk, (j + 1) * blk)
                kb = k_ref[rows, lanes(hh)].astype(F32)
                kmean_ref[hh, j:j + 1, :] = jnp.sum(kb, axis=0, keepdims=True) * (1.0 / blk)
                vt_ref[hh, 0:HEAD_DIM, rows] = v_ref[rows, lanes(hh)].astype(F32).T.astype(BF16)

    q = [q_ref[:, lanes(hh)] for hh in range(heads)]
    q_t = [qh.astype(F32).T.astype(BF16) for qh in q]

    def stage_a(g, hh, buf):
        s_ref, bmax_ref = buf
        start = pl.multiple_of(g * gk, gk)
        s_g = jnp.dot(k_ref[pl.ds(start, gk), lanes(hh)], q_t[hh], preferred_element_type=F32)
        s_ref[hh] = s_g
        for u in range(group):
            bmax_ref[hh, u:u + 1, :] = jnp.max(s_g[u * blk:(u + 1) * blk], axis=0, keepdims=True)

    def stage_a_own(hh, buf):
        s_ref, bmax_ref = buf
        start = pl.multiple_of(i * blk, blk)
        kpos = lax.broadcasted_iota(jnp.int32, (blk, blk), 0)
        qpos = lax.broadcasted_iota(jnp.int32, (blk, blk), 1)
        s_i = jnp.dot(k_ref[pl.ds(start, blk), lanes(hh)], q_t[hh], preferred_element_type=F32)
        s_i = jnp.where(kpos <= qpos, s_i, NEG_BIG)
        s_ref[hh, 0:blk, :] = s_i
        bmax_ref[hh, 0:1, :] = jnp.max(s_i, axis=0, keepdims=True)

    def stage_b(first_blk, n_u, hh, buf, picks):
        s_ref, bmax_ref = buf
        start = pl.multiple_of(first_blk * blk, blk)
        m = stat_ref[hh, 0:1, :]
        chosen, m_new = [], m
        for u in range(n_u):
            b_u = bmax_ref[hh, u:u + 1, :]
            if picks is not None:
                ch = picks[0] == first_blk + u
                for s in picks[1:]:
                    ch = ch | (s == first_blk + u)
                chosen.append(ch)
                b_u = jnp.where(ch, b_u, NEG_BIG)
            m_new = jnp.maximum(m_new, b_u)
        for u in range(n_u):
            shift = m_new if picks is None else jnp.where(chosen[u], m_new, -NEG_BIG)
            p_u = jnp.exp2(s_ref[hh, u * blk:(u + 1) * blk, :] - shift)
            p_ref[hh, u * blk:(u + 1) * blk, :] = p_u.astype(BF16)
        alpha = jnp.exp2(m - m_new)
        pv = jnp.dot(vt_ref[hh, :, pl.ds(start, n_u * blk)], p_ref[hh, 0:n_u * blk, :],
                     preferred_element_type=F32)
        acc_ref[hh] = alpha * acc_ref[hh] + pv[0:HEAD_DIM]
        stat_ref[hh, 0:1, :] = m_new
        stat_ref[hh, 1:2, :] = alpha * stat_ref[hh, 1:2, :] + pv[HEAD_DIM:HEAD_DIM + 1]

    for hh in range(heads):
        stage_a(0, hh, buf0)

    sel = []
    for hh in range(heads):
        gate = lax.dot_general(kmean_ref[hh].astype(BF16), q[hh], nt, preferred_element_type=F32)
        blk_iota = lax.broadcasted_iota(jnp.int32, gate.shape, 0)
        gate = jnp.where(blk_iota < i, gate, -jnp.inf)
        picks = []
        for _ in range(MOBA_TOP_K):
            mx = jnp.max(gate, axis=0, keepdims=True)
            idx = jnp.min(jnp.where(gate == mx, blk_iota, n_pad), axis=0, keepdims=True)
            picks.append(jnp.where(mx > -jnp.inf, idx, -1))
            gate = jnp.where(blk_iota == idx, -jnp.inf, gate)
        sel.append(picks)

    acc_ref[...] = jnp.zeros_like(acc_ref)
    stat_ref[:, 0:1, :] = jnp.full((heads, 1, blk), NEG_BIG, F32)
    stat_ref[:, 1:2, :] = jnp.zeros((heads, 1, blk), F32)
    n_groups = (i + group - 1) // group
    last = jnp.maximum(n_groups - 1, 0)

    def steady(g, rd, wr):
        for hh in range(heads):
            stage_a(g + 1, hh, wr)
            stage_b(g * group, group, hh, rd, sel[hh])

    def drain(rd, wr):
        for hh in range(heads):
            stage_a_own(hh, wr)
            stage_b(last * group, group, hh, rd, sel[hh])
        for hh in range(heads):
            stage_b(i, 1, hh, wr, None)

    def body(g, carry):
        pl.when(g % 2 == 0)(functools.partial(steady, g, buf0, buf1))
        pl.when(g % 2 == 1)(functools.partial(steady, g, buf1, buf0))
        return carry

    lax.fori_loop(0, last, body, 0)
    pl.when(last % 2 == 0)(functools.partial(drain, buf0, buf1))
    pl.when(last % 2 == 1)(functools.partial(drain, buf1, buf0))

    for hh in range(heads):
        out = (acc_ref[hh] * (1.0 / stat_ref[hh, 1:2, :])).T
        o_ref[:, lanes(hh)] = (out * za_ref[:, lanes(hh)].astype(F32)).astype(o_ref.dtype)


def _attention(h, *, batch, seq, d_attn):
    m = h.shape[0]
    n_heads = d_attn // HEAD_DIM
    blk = MOBA_BLOCK
    assert seq % blk == 0
    n_blk = seq // blk
    n_pad = -(-n_blk // 8) * 8
    group = math.gcd(n_blk, ATTN_GROUP)
    assert group <= 8
    heads = math.gcd(n_heads, ATTN_HEADS_PER_STEP)
    hw = heads * HEAD_DIM
    col0 = n_heads // heads
    kern = functools.partial(_attn_kernel, n_blk=n_blk, blk=blk, group=group)
    return pl.pallas_call(
        kern,
        grid=(batch, n_heads // heads, n_blk),
        in_specs=[pl.BlockSpec((blk, hw), lambda b, hp, i: (b * n_blk + i, hp)),
                  pl.BlockSpec((seq, hw), lambda b, hp, i: (b, col0 + hp)),
                  pl.BlockSpec((seq, hw), lambda b, hp, i: (b, 2 * col0 + hp)),
                  pl.BlockSpec((blk, hw), lambda b, hp, i: (b * n_blk + i, 3 * col0 + hp))],
        out_specs=pl.BlockSpec((blk, hw), lambda b, hp, i: (b * n_blk + i, hp)),
        out_shape=jax.ShapeDtypeStruct((m, d_attn), BF16),
        scratch_shapes=[pltpu.VMEM((heads, n_pad, HEAD_DIM), F32),
                        pltpu.VMEM((heads, HEAD_DIM + 16, seq), BF16),
                        pltpu.VMEM((heads, group * blk, blk), F32),
                        pltpu.VMEM((heads, group * blk, blk), F32),
                        pltpu.VMEM((heads, 8, blk), F32),
                        pltpu.VMEM((heads, 8, blk), F32),
                        pltpu.VMEM((heads, group * blk, blk), BF16),
                        pltpu.VMEM((heads, HEAD_DIM, blk), F32),
                        pltpu.VMEM((heads, 8, blk), F32)],
        compiler_params=_compiler_params(3),
        name="moba_attention",
    )(h, h, h, h)


def _conv_kernel(cv_ref, cg_ref, hv_ref, hg_ref, zc_ref, w_ref, b_ref, g_ref, beta_ref, o_ref,
                 ext_ref, conv_ref, *, tiles_per_seq, row_chunk, ch_chunk):
    tt, c = cv_ref.shape
    width = w_ref.shape[0]
    first = (pl.program_id(0) % tiles_per_seq) == 0
    halo = hv_ref[...].astype(F32) * _sigmoid(hg_ref[...].astype(F32))
    ext_ref[0:CONV_HALO, :] = jnp.where(first, 0.0, halo)
    ext_ref[CONV_HALO:, :] = cv_ref[...].astype(F32) * _sigmoid(cg_ref[...].astype(F32))
    off = CONV_HALO - (width - 1)
    for r0 in range(0, tt, row_chunk):
        for c0 in range(0, c, ch_chunk):
            acc = jnp.zeros((row_chunk, ch_chunk), F32) + b_ref[:, c0:c0 + ch_chunk]
            for k in range(width):
                acc = acc + (w_ref[k:k + 1, c0:c0 + ch_chunk]
                             * ext_ref[r0 + off + k:r0 + off + k + row_chunk, c0:c0 + ch_chunk])
            conv_ref[r0:r0 + row_chunk, c0:c0 + ch_chunk] = acc
    u = conv_ref[...]
    mu = jnp.mean(u, axis=-1, keepdims=True)
    d = u - mu
    var = jnp.mean(d * d, axis=-1, keepdims=True)
    n = d * lax.rsqrt(var + LN_EPS) * g_ref[...] + beta_ref[...]
    o_ref[...] = (n * _sigmoid(n) * zc_ref[...].astype(F32)).astype(o_ref.dtype)


def _conv_branch(h, w_conv, b_conv, ln_g, ln_b, *, seq, d_attn, d_conv):
    m = h.shape[0]
    c = d_conv
    width = w_conv.shape[0]
    assert width - 1 <= CONV_HALO and (4 * d_attn) % c == 0
    tt = _tile(seq, 256)
    assert tt % CONV_HALO == 0
    cv_blk = (4 * d_attn) // c
    tiles_per_seq = seq // tt
    hpt = tt // CONV_HALO
    kern = functools.partial(_conv_kernel, tiles_per_seq=tiles_per_seq,
                             row_chunk=_tile(tt, 32), ch_chunk=_tile(c, 512))
    row = lambda a: a.reshape(1, c)
    halo_idx = lambda i: jnp.maximum(i * hpt - 1, 0)
    vec_spec = pl.BlockSpec((1, c), lambda i: (0, 0))
    return pl.pallas_call(
        kern,
        grid=(m // tt,),
        in_specs=[pl.BlockSpec((tt, c), lambda i: (i, cv_blk)),
                  pl.BlockSpec((tt, c), lambda i: (i, cv_blk + 1)),
                  pl.BlockSpec((CONV_HALO, c), lambda i: (halo_idx(i), cv_blk)),
                  pl.BlockSpec((CONV_HALO, c), lambda i: (halo_idx(i), cv_blk + 1)),
                  pl.BlockSpec((tt, c), lambda i: (i, cv_blk + 2)),
                  pl.BlockSpec((width, c), lambda i: (0, 0)),
                  vec_spec, vec_spec, vec_spec],
        out_specs=pl.BlockSpec((tt, c), lambda i: (i, 0)),
        out_shape=jax.ShapeDtypeStruct((m, c), BF16),
        scratch_shapes=[pltpu.VMEM((tt + CONV_HALO, c), F32), pltpu.VMEM((tt, c), F32)],
        compiler_params=_compiler_params(1),
        name="conv_branch",
    )(h, h, h, h, h, w_conv, row(b_conv), row(ln_g), row(ln_b))


def _merge_kernel(a_ref, c_ref, woa_ref, woc_ref, ga_ref, gc_ref, o_ref):
    oa = jnp.dot(a_ref[...], woa_ref[...], preferred_element_type=F32)
    oc = jnp.dot(c_ref[...], woc_ref[...], preferred_element_type=F32)
    o_ref[...] = (ga_ref[...].astype(F32) * oa + gc_ref[...].astype(F32) * oc).astype(o_ref.dtype)


def _merge(a, cbr, h, woa16, woc16, *, d_attn, d_conv):
    m = a.shape[0]
    d = woa16.shape[1]
    gates0 = 4 * d_attn + 3 * d_conv
    tm, tn = _tile(m, 1024), _tile(math.gcd(d, gates0), 1024)
    ga_blk, gc_blk = gates0 // tn, (gates0 + d) // tn
    return pl.pallas_call(
        _merge_kernel,
        grid=(d // tn, m // tm),
        in_specs=[pl.BlockSpec((tm, d_attn), lambda j, i: (i, 0)),
                  pl.BlockSpec((tm, d_conv), lambda j, i: (i, 0)),
                  pl.BlockSpec((d_attn, tn), lambda j, i: (0, j)),
                  pl.BlockSpec((d_conv, tn), lambda j, i: (0, j)),
                  pl.BlockSpec((tm, tn), lambda j, i: (i, ga_blk + j)),
                  pl.BlockSpec((tm, tn), lambda j, i: (i, gc_blk + j))],
        out_specs=pl.BlockSpec((tm, tn), lambda j, i: (i, j)),
        out_shape=jax.ShapeDtypeStruct((m, d), BF16),
        compiler_params=_compiler_params(2),
        name="merge",
    )(a, cbr, woa16, woc16, h, h)


def _out_proj_kernel(mg_ref, w_ref, x_ref, y32_ref, y16_ref, *, alpha):
    y = alpha * x_ref[...] + jnp.dot(mg_ref[...], w_ref[...], preferred_element_type=F32)
    y32_ref[...] = y
    y16_ref[...] = y.astype(y16_ref.dtype)


def _out_proj(merged, w16, x32, *, alpha):
    m, d = x32.shape
    tm, tn = _tile(m, 1024), _tile(d, 512)
    tile = pl.BlockSpec((tm, tn), lambda j, i: (i, j))
    return pl.pallas_call(
        functools.partial(_out_proj_kernel, alpha=alpha),
        grid=(d // tn, m // tm),
        in_specs=[pl.BlockSpec((tm, d), lambda j, i: (i, 0)),
                  pl.BlockSpec((d, tn), lambda j, i: (0, j)),
                  tile],
        out_specs=[tile, tile],
        out_shape=[jax.ShapeDtypeStruct((m, d), F32), jax.ShapeDtypeStruct((m, d), BF16)],
        compiler_params=_compiler_params(2),
        name="out_proj",
    )(merged, w16, x32)


def _ple_kernel(y16_ref, wpg_ref, bpg_ref, p_ref, wpu_ref, y32_ref, z_ref):
    gate = jnp.dot(y16_ref[...], wpg_ref[...], preferred_element_type=F32) + bpg_ref[...]
    up = jnp.dot(p_ref[...], wpu_ref[...], preferred_element_type=F32)
    z_ref[...] = y32_ref[...] + _sigmoid(gate) * up


def _ple(y32, y16, p16, wpg16, bpg, wpu16):
    m, d = y32.shape
    pd = p16.shape[1]
    tm, tn = _tile(m, 1024), _tile(d, 512)
    return pl.pallas_call(
        _ple_kernel,
        grid=(d // tn, m // tm),
        in_specs=[pl.BlockSpec((tm, d), lambda j, i: (i, 0)),
                  pl.BlockSpec((d, tn), lambda j, i: (0, j)),
                  pl.BlockSpec((1, tn), lambda j, i: (0, j)),
                  pl.BlockSpec((tm, pd), lambda j, i: (i, 0)),
                  pl.BlockSpec((pd, tn), lambda j, i: (0, j)),
                  pl.BlockSpec((tm, tn), lambda j, i: (i, j))],
        out_specs=pl.BlockSpec((tm, tn), lambda j, i: (i, j)),
        out_shape=jax.ShapeDtypeStruct((m, d), F32),
        compiler_params=_compiler_params(2),
        name="ple",
    )(y16, wpg16, bpg.reshape(1, d), p16, wpu16, y32)


def _ln_kernel(z_ref, g_ref, b_ref, o32_ref, o16_ref):
    z = z_ref[...]
    mu = jnp.mean(z, axis=-1, keepdims=True)
    dz = z - mu
    var = jnp.mean(dz * dz, axis=-1, keepdims=True)
    out = dz * lax.rsqrt(var + LN_EPS) * g_ref[...] + b_ref[...]
    o32_ref[...] = out
    o16_ref[...] = out.astype(o16_ref.dtype)


def _layer_norm(z, g, b):
    m, d = z.shape
    tr = _tile(m, 256)
    tile = pl.BlockSpec((tr, d), lambda i: (i, 0))
    vec = pl.BlockSpec((1, d), lambda i: (0, 0))
    return pl.pallas_call(
        _ln_kernel,
        grid=(m // tr,),
        in_specs=[tile, vec, vec],
        out_specs=[tile, tile],
        out_shape=[jax.ShapeDtypeStruct((m, d), F32), jax.ShapeDtypeStruct((m, d), BF16)],
        compiler_params=_compiler_params(1),
        name="layer_norm",
    )(z, g.reshape(1, d), b.reshape(1, d))


def kernel(x, p, w_in, b_in, w_conv, b_conv, conv_ln_g, conv_ln_b, w_o_attn, w_o_conv, w_out,
           w_ple_up, w_ple_gate, b_ple_gate, ln_g, ln_b):
    batch, seq, d_model = x.shape
    depth = w_in.shape[0]
    d_attn = w_o_attn.shape[1]
    d_conv = w_o_conv.shape[1]
    assert w_in.shape[2] == 4 * d_attn + 3 * d_conv + 2 * d_model
    alpha = (2.0 * depth) ** 0.25
    m = batch * seq
    x32 = x.reshape(m, d_model)
    x16 = x32.astype(BF16)
    for i in range(depth):
        h = _in_proj(x16, w_in[i].astype(BF16), b_in[i], d_attn=d_attn, d_conv=d_conv)
        a = _attention(h, batch=batch, seq=seq, d_attn=d_attn)
        cbr = _conv_branch(h, w_conv[i], b_conv[i], conv_ln_g[i], conv_ln_b[i],
                           seq=seq, d_attn=d_attn, d_conv=d_conv)
        merged = _merge(a, cbr, h, w_o_attn[i].astype(BF16), w_o_conv[i].astype(BF16),
                        d_attn=d_attn, d_conv=d_conv)
        y32, y16 = _out_proj(merged, w_out[i].astype(BF16), x32, alpha=alpha)
        z = _ple(y32, y16, p[i].reshape(m, -1).astype(BF16), w_ple_gate[i].astype(BF16),
                 b_ple_gate[i], w_ple_up[i].astype(BF16))
        x32, x16 = _layer_norm(z, ln_g[i], ln_b[i])
    return x32.reshape(batch, seq, d_model)
```

```python
import functools
import math

import jax
import jax.numpy as jnp
from jax import lax
from jax.experimental import pallas as pl
from jax.experimental.pallas import tpu as pltpu

HEAD_DIM = 128
SUBLANES = 8
MOBA_BLOCK = 256
MOBA_TOP_K = 3
LN_EPS = 1e-5
CONV_HALO = 32
NEG_BIG = -1e30
Q_SCALE = HEAD_DIM ** -0.5 * math.log2(math.e)
ATTN_GROUP = 4
ATTN_HEADS_PER_STEP = 2
ATTN_Q_BLOCKS = 4
V7X_VMEM_LIMIT_BYTES = 56 * 1024 * 1024

F32 = jnp.float32
BF16 = jnp.bfloat16


def _compiler_params(n_axes):
    return pltpu.CompilerParams(dimension_semantics=("arbitrary",) * n_axes,
                                vmem_limit_bytes=V7X_VMEM_LIMIT_BYTES)


def _tile(dim, want):
    t = min(dim, want)
    assert dim % t == 0, (dim, want)
    return t


def _sigmoid(x):
    return 1.0 / (1.0 + jnp.exp(-x))


def _in_proj_kernel(x_ref, w_ref, b_ref, o_ref, *, tn, d_attn, d_conv, q_scale):
    col = pl.program_id(0) * tn
    acc = jnp.dot(x_ref[...], w_ref[...], preferred_element_type=F32) + b_ref[...]
    z_a0 = 3 * d_attn
    z_c0 = 4 * d_attn + 2 * d_conv
    gates0 = z_c0 + d_conv
    is_q = col < d_attn
    is_silu = ((col >= z_a0) & (col < z_a0 + d_attn)) | ((col >= z_c0) & (col < gates0))
    is_sig = col >= gates0
    is_plain = jnp.logical_not(is_q | is_silu | is_sig)

    @pl.when(is_q)
    def _():
        o_ref[...] = (acc * q_scale).astype(o_ref.dtype)

    @pl.when(is_plain)
    def _():
        o_ref[...] = acc.astype(o_ref.dtype)

    @pl.when(is_silu)
    def _():
        o_ref[...] = (acc * _sigmoid(acc)).astype(o_ref.dtype)

    @pl.when(is_sig)
    def _():
        o_ref[...] = _sigmoid(acc).astype(o_ref.dtype)


def _in_proj(x16, w16, b, *, d_attn, d_conv):
    m, k = x16.shape
    n = w16.shape[1]
    tm, tn = _tile(m, 1024), _tile(d_attn, 1024)
    assert n % tn == 0 and d_conv % tn == 0
    kern = functools.partial(_in_proj_kernel, tn=tn, d_attn=d_attn, d_conv=d_conv,
                             q_scale=Q_SCALE)
    return pl.pallas_call(
        kern,
        grid=(n // tn, m // tm),
        in_specs=[pl.BlockSpec((tm, k), lambda j, i: (i, 0)),
                  pl.BlockSpec((k, tn), lambda j, i: (0, j)),
                  pl.BlockSpec((1, tn), lambda j, i: (0, j))],
        out_specs=pl.BlockSpec((tm, tn), lambda j, i: (i, j)),
        out_shape=jax.ShapeDtypeStruct((m, n), BF16),
        compiler_params=_compiler_params(2),
        name="in_proj",
    )(x16, w16, b.reshape(1, n))


def _attn_kernel(q_ref, k_ref, v_ref, za_ref, o_ref, kmean_ref, vt_ref, s0_ref, s1_ref, bmax0_ref,
                 bmax1_ref, p_ref, acc_ref, stat_ref, *, n_blk, blk, group, q_blocks):
    i0 = pl.program_id(2) * q_blocks
    heads = q_ref.shape[1] // HEAD_DIM
    nq = q_blocks * blk
    n_pad = kmean_ref.shape[1]
    gk = group * blk
    nt = (((1,), (1,)), ((), ()))
    buf0, buf1 = (s0_ref, bmax0_ref), (s1_ref, bmax1_ref)

    def lanes(hh):
        return slice(hh * HEAD_DIM, (hh + 1) * HEAD_DIM)

    @pl.when(i0 == 0)
    def _():
        kmean_ref[...] = jnp.zeros_like(kmean_ref)
        vt_ref[:, HEAD_DIM:, :] = jnp.ones((heads, vt_ref.shape[1] - HEAD_DIM, vt_ref.shape[2]), BF16)
        for hh in range(heads):
            for j in range(n_blk):
                rows = slice(j * blk, (j + 1) * blk)
                kb = k_ref[rows, lanes(hh)].astype(F32)
                kmean_ref[hh, j:j + 1, :] = jnp.sum(kb, axis=0, keepdims=True) * (1.0 / blk)
                vt_ref[hh, 0:HEAD_DIM, rows] = v_ref[rows, lanes(hh)].astype(F32).T.astype(BF16)

    q = [q_ref[:, lanes(hh)] for hh in range(heads)]
    q_t = [qh.astype(F32).T.astype(BF16) for qh in q]
    lane = lax.broadcasted_iota(jnp.int32, (1, nq), 1)
    own = i0 + sum((lane >= b * blk).astype(jnp.int32) for b in range(1, q_blocks))
    n_full = i0 // group

    def scores_to(buf, hh, s_g):
        s_ref, bmax_ref = buf
        s_ref[hh] = s_g
        for u in range(group):
            bmax_ref[hh, u:u + 1, :] = jnp.max(s_g[u * blk:(u + 1) * blk], axis=0, keepdims=True)

    def stage_a(g, hh, buf):
        start = pl.multiple_of(g * gk, gk)
        scores_to(buf, hh, jnp.dot(k_ref[pl.ds(start, gk), lanes(hh)], q_t[hh],
                                   preferred_element_type=F32))

    def stage_a_tail(hh, buf):
        start = pl.multiple_of(n_full * gk, gk)
        s_g = jnp.dot(k_ref[pl.ds(start, gk), lanes(hh)], q_t[hh], preferred_element_type=F32)
        kpos = lax.broadcasted_iota(jnp.int32, (gk, 1), 0) + (n_full * group - i0) * blk
        scores_to(buf, hh, jnp.where(kpos <= lane, s_g, NEG_BIG))

    def stage_b(first_blk, hh, buf, picks, *, tail):
        s_ref, bmax_ref = buf
        start = pl.multiple_of(first_blk * blk, blk)
        m = stat_ref[hh, 0:1, :]
        chosen, m_new = [], m
        for u in range(group):
            ch = picks[0] == first_blk + u
            for s in picks[1:]:
                ch = ch | (s == first_blk + u)
            if tail:
                ch = ch | (own == first_blk + u)
            chosen.append(ch)
            m_new = jnp.maximum(m_new, jnp.where(ch, bmax_ref[hh, u:u + 1, :], NEG_BIG))
        for u in range(group):
            shift = jnp.where(chosen[u], m_new, -NEG_BIG)
            p_u = jnp.exp2(s_ref[hh, u * blk:(u + 1) * blk, :] - shift)
            p_ref[hh, u * blk:(u + 1) * blk, :] = p_u.astype(BF16)
        alpha = jnp.exp2(m - m_new)
        pv = jnp.dot(vt_ref[hh, :, pl.ds(start, gk)], p_ref[hh], preferred_element_type=F32)
        acc_ref[hh] = alpha * acc_ref[hh] + pv[0:HEAD_DIM]
        stat_ref[hh, 0:1, :] = m_new
        stat_ref[hh, 1:2, :] = alpha * stat_ref[hh, 1:2, :] + pv[HEAD_DIM:HEAD_DIM + 1]

    for hh in range(heads):
        stage_a(0, hh, buf0)

    sel = []
    for hh in range(heads):
        gate = lax.dot_general(kmean_ref[hh].astype(BF16), q[hh], nt, preferred_element_type=F32)
        blk_iota = lax.broadcasted_iota(jnp.int32, gate.shape, 0)
        gate = jnp.where(blk_iota < own, gate, -jnp.inf)
        picks = []
        for _ in range(MOBA_TOP_K):
            mx = jnp.max(gate, axis=0, keepdims=True)
            idx = jnp.min(jnp.where(gate == mx, blk_iota, n_pad), axis=0, keepdims=True)
            picks.append(jnp.where(mx > -jnp.inf, idx, -1))
            gate = jnp.where(blk_iota == idx, -jnp.inf, gate)
        sel.append(picks)

    acc_ref[...] = jnp.zeros_like(acc_ref)
    stat_ref[:, 0:1, :] = jnp.full((heads, 1, nq), NEG_BIG, F32)
    stat_ref[:, 1:2, :] = jnp.zeros((heads, 1, nq), F32)
    last = jnp.maximum(n_full - 1, 0)

    def steady(g, rd, wr):
        for hh in range(heads):
            stage_a(g + 1, hh, wr)
            stage_b(g * group, hh, rd, sel[hh], tail=False)

    def drain(rd, wr):
        @pl.when(n_full > 0)
        def _():
            for hh in range(heads):
                stage_a_tail(hh, wr)
                stage_b(last * group, hh, rd, sel[hh], tail=False)

        @pl.when(n_full == 0)
        def _():
            for hh in range(heads):
                stage_a_tail(hh, wr)

        for hh in range(heads):
            stage_b(n_full * group, hh, wr, sel[hh], tail=True)

    def body(g, carry):
        pl.when(g % 2 == 0)(functools.partial(steady, g, buf0, buf1))
        pl.when(g % 2 == 1)(functools.partial(steady, g, buf1, buf0))
        return carry

    lax.fori_loop(0, last, body, 0)
    pl.when(last % 2 == 0)(functools.partial(drain, buf0, buf1))
    pl.when(last % 2 == 1)(functools.partial(drain, buf1, buf0))

    for hh in range(heads):
        out = (acc_ref[hh] * (1.0 / stat_ref[hh, 1:2, :])).T
        o_ref[:, lanes(hh)] = (out * za_ref[:, lanes(hh)].astype(F32)).astype(o_ref.dtype)


def _attention(h, *, batch, seq, d_attn):
    m = h.shape[0]
    n_heads = d_attn // HEAD_DIM
    blk = MOBA_BLOCK
    assert seq % blk == 0
    n_blk = seq // blk
    n_pad = -(-n_blk // 8) * 8
    group = math.gcd(n_blk, ATTN_GROUP)
    q_blocks = math.gcd(group, ATTN_Q_BLOCKS)
    assert group <= 8
    heads = math.gcd(n_heads, ATTN_HEADS_PER_STEP)
    hw = heads * HEAD_DIM
    nq = q_blocks * blk
    n_steps = n_blk // q_blocks
    col0 = n_heads // heads
    kern = functools.partial(_attn_kernel, n_blk=n_blk, blk=blk, group=group, q_blocks=q_blocks)
    return pl.pallas_call(
        kern,
        grid=(batch, n_heads // heads, n_steps),
        in_specs=[pl.BlockSpec((nq, hw), lambda b, hp, i: (b * n_steps + i, hp)),
                  pl.BlockSpec((seq, hw), lambda b, hp, i: (b, col0 + hp)),
                  pl.BlockSpec((seq, hw), lambda b, hp, i: (b, 2 * col0 + hp)),
                  pl.BlockSpec((nq, hw), lambda b, hp, i: (b * n_steps + i, 3 * col0 + hp))],
        out_specs=pl.BlockSpec((nq, hw), lambda b, hp, i: (b * n_steps + i, hp)),
        out_shape=jax.ShapeDtypeStruct((m, d_attn), BF16),
        scratch_shapes=[pltpu.VMEM((heads, n_pad, HEAD_DIM), F32),
                        pltpu.VMEM((heads, HEAD_DIM + 16, seq), BF16),
                        pltpu.VMEM((heads, group * blk, nq), F32),
                        pltpu.VMEM((heads, group * blk, nq), F32),
                        pltpu.VMEM((heads, 8, nq), F32),
                        pltpu.VMEM((heads, 8, nq), F32),
                        pltpu.VMEM((heads, group * blk, nq), BF16),
                        pltpu.VMEM((heads, HEAD_DIM, nq), F32),
                        pltpu.VMEM((heads, 8, nq), F32)],
        compiler_params=_compiler_params(3),
        name="moba_attention",
    )(h, h, h, h)


def _conv_kernel(cv_ref, cg_ref, hv_ref, hg_ref, zc_ref, w_ref, b_ref, g_ref, beta_ref, o_ref,
                 ext_ref, shift_ref, wb_ref, conv_ref, *, tiles_per_seq, row_chunk, ch_chunk):
    tt, c = cv_ref.shape
    width = w_ref.shape[0]
    first = (pl.program_id(0) % tiles_per_seq) == 0
    halo = hv_ref[...].astype(F32) * _sigmoid(hg_ref[...].astype(F32))
    ext_ref[0:CONV_HALO, :] = jnp.where(first, 0.0, halo)
    ext_ref[CONV_HALO:, :] = cv_ref[...].astype(F32) * _sigmoid(cg_ref[...].astype(F32))
    off = CONV_HALO - (width - 1)
    n_shift_rows = shift_ref.shape[1]
    for k in range(width):
        wb_ref[k] = jnp.broadcast_to(w_ref[k:k + 1, :], (SUBLANES, c))
    for c0 in range(0, c, ch_chunk):
        cols = slice(c0, c0 + ch_chunk)
        for s in range(1, SUBLANES):
            shift_ref[s - 1] = ext_ref[s:s + n_shift_rows, cols]
        for r0 in range(0, tt, row_chunk):
            acc = jnp.zeros((row_chunk // SUBLANES, SUBLANES, ch_chunk), F32) + b_ref[:, cols]
            for k in range(width):
                s = (off + k) % SUBLANES
                base = r0 + (off + k) - s
                if s == 0:
                    window = ext_ref[base:base + row_chunk, cols]
                else:
                    window = shift_ref[s - 1, base:base + row_chunk, :]
                acc = acc + wb_ref[k, :, cols] * window.reshape(acc.shape)
            conv_ref[r0:r0 + row_chunk, cols] = acc.reshape(row_chunk, ch_chunk)
    u = conv_ref[...]
    mu = jnp.mean(u, axis=-1, keepdims=True)
    d = u - mu
    var = jnp.mean(d * d, axis=-1, keepdims=True)
    n = d * lax.rsqrt(var + LN_EPS) * g_ref[...] + beta_ref[...]
    o_ref[...] = (n * _sigmoid(n) * zc_ref[...].astype(F32)).astype(o_ref.dtype)


def _conv_branch(h, w_conv, b_conv, ln_g, ln_b, *, seq, d_attn, d_conv):
    m = h.shape[0]
    c = d_conv
    width = w_conv.shape[0]
    assert width - 1 <= CONV_HALO and (4 * d_attn) % c == 0
    tt = _tile(seq, 256)
    assert tt % CONV_HALO == 0
    cv_blk = (4 * d_attn) // c
    tiles_per_seq = seq // tt
    hpt = tt // CONV_HALO
    ch_chunk = _tile(c, 512)
    kern = functools.partial(_conv_kernel, tiles_per_seq=tiles_per_seq,
                             row_chunk=_tile(tt, 32), ch_chunk=ch_chunk)
    row = lambda a: a.reshape(1, c)
    halo_idx = lambda i: jnp.maximum(i * hpt - 1, 0)
    vec_spec = pl.BlockSpec((1, c), lambda i: (0, 0))
    return pl.pallas_call(
        kern,
        grid=(m // tt,),
        in_specs=[pl.BlockSpec((tt, c), lambda i: (i, cv_blk)),
                  pl.BlockSpec((tt, c), lambda i: (i, cv_blk + 1)),
                  pl.BlockSpec((CONV_HALO, c), lambda i: (halo_idx(i), cv_blk)),
                  pl.BlockSpec((CONV_HALO, c), lambda i: (halo_idx(i), cv_blk + 1)),
                  pl.BlockSpec((tt, c), lambda i: (i, cv_blk + 2)),
                  pl.BlockSpec((width, c), lambda i: (0, 0)),
                  vec_spec, vec_spec, vec_spec],
        out_specs=pl.BlockSpec((tt, c), lambda i: (i, 0)),
        out_shape=jax.ShapeDtypeStruct((m, c), BF16),
        scratch_shapes=[pltpu.VMEM((tt + CONV_HALO, c), F32),
                        pltpu.VMEM((SUBLANES - 1, tt + CONV_HALO - SUBLANES, ch_chunk), F32),
                        pltpu.VMEM((width, SUBLANES, c), F32),
                        pltpu.VMEM((tt, c), F32)],
        compiler_params=_compiler_params(1),
        name="conv_branch",
    )(h, h, h, h, h, w_conv, row(b_conv), row(ln_g), row(ln_b))


def _merge_kernel(a_ref, c_ref, woa_ref, woc_ref, ga_ref, gc_ref, o_ref):
    oa = jnp.dot(a_ref[...], woa_ref[...], preferred_element_type=F32)
    oc = jnp.dot(c_ref[...], woc_ref[...], preferred_element_type=F32)
    o_ref[...] = (ga_ref[...].astype(F32) * oa + gc_ref[...].astype(F32) * oc).astype(o_ref.dtype)


def _merge(a, cbr, h, woa16, woc16, *, d_attn, d_conv):
    m = a.shape[0]
    d = woa16.shape[1]
    gates0 = 4 * d_attn + 3 * d_conv
    tm, tn = _tile(m, 1024), _tile(math.gcd(d, gates0), 1024)
    ga_blk, gc_blk = gates0 // tn, (gates0 + d) // tn
    return pl.pallas_call(
        _merge_kernel,
        grid=(d // tn, m // tm),
        in_specs=[pl.BlockSpec((tm, d_attn), lambda j, i: (i, 0)),
                  pl.BlockSpec((tm, d_conv), lambda j, i: (i, 0)),
                  pl.BlockSpec((d_attn, tn), lambda j, i: (0, j)),
                  pl.BlockSpec((d_conv, tn), lambda j, i: (0, j)),
                  pl.BlockSpec((tm, tn), lambda j, i: (i, ga_blk + j)),
                  pl.BlockSpec((tm, tn), lambda j, i: (i, gc_blk + j))],
        out_specs=pl.BlockSpec((tm, tn), lambda j, i: (i, j)),
        out_shape=jax.ShapeDtypeStruct((m, d), BF16),
        compiler_params=_compiler_params(2),
        name="merge",
    )(a, cbr, woa16, woc16, h, h)


def _out_proj_kernel(mg_ref, w_ref, x_ref, y32_ref, y16_ref, *, alpha):
    y = alpha * x_ref[...] + jnp.dot(mg_ref[...], w_ref[...], preferred_element_type=F32)
    y32_ref[...] = y
    y16_ref[...] = y.astype(y16_ref.dtype)


def _out_proj(merged, w16, x32, *, alpha):
    m, d = x32.shape
    tm, tn = _tile(m, 1024), _tile(d, 512)
    tile = pl.BlockSpec((tm, tn), lambda j, i: (i, j))
    return pl.pallas_call(
        functools.partial(_out_proj_kernel, alpha=alpha),
        grid=(d // tn, m // tm),
        in_specs=[pl.BlockSpec((tm, d), lambda j, i: (i, 0)),
                  pl.BlockSpec((d, tn), lambda j, i: (0, j)),
                  tile],
        out_specs=[tile, tile],
        out_shape=[jax.ShapeDtypeStruct((m, d), F32), jax.ShapeDtypeStruct((m, d), BF16)],
        compiler_params=_compiler_params(2),
        name="out_proj",
    )(merged, w16, x32)


def _ple_kernel(y16_ref, wpg_ref, bpg_ref, p_ref, wpu_ref, y32_ref, z_ref):
    gate = jnp.dot(y16_ref[...], wpg_ref[...], preferred_element_type=F32) + bpg_ref[...]
    up = jnp.dot(p_ref[...].astype(BF16), wpu_ref[...], preferred_element_type=F32)
    z_ref[...] = y32_ref[...] + _sigmoid(gate) * up


def _ple(y32, y16, p32, wpg16, bpg, wpu16):
    m, d = y32.shape
    pd = p32.shape[1]
    tm, tn = _tile(m, 1024), _tile(d, 512)
    return pl.pallas_call(
        _ple_kernel,
        grid=(d // tn, m // tm),
        in_specs=[pl.BlockSpec((tm, d), lambda j, i: (i, 0)),
                  pl.BlockSpec((d, tn), lambda j, i: (0, j)),
                  pl.BlockSpec((1, tn), lambda j, i: (0, j)),
                  pl.BlockSpec((tm, pd), lambda j, i: (i, 0)),
                  pl.BlockSpec((pd, tn), lambda j, i: (0, j)),
                  pl.BlockSpec((tm, tn), lambda j, i: (i, j))],
        out_specs=pl.BlockSpec((tm, tn), lambda j, i: (i, j)),
        out_shape=jax.ShapeDtypeStruct((m, d), F32),
        compiler_params=_compiler_params(2),
        name="ple",
    )(y16, wpg16, bpg.reshape(1, d), p32, wpu16, y32)


def _ln_kernel(z_ref, g_ref, b_ref, o32_ref, o16_ref):
    z = z_ref[...]
    mu = jnp.mean(z, axis=-1, keepdims=True)
    dz = z - mu
    var = jnp.mean(dz * dz, axis=-1, keepdims=True)
    out = dz * lax.rsqrt(var + LN_EPS) * g_ref[...] + b_ref[...]
    o32_ref[...] = out
    o16_ref[...] = out.astype(o16_ref.dtype)


def _layer_norm(z, g, b):
    m, d = z.shape
    tr = _tile(m, 256)
    tile = pl.BlockSpec((tr, d), lambda i: (i, 0))
    vec = pl.BlockSpec((1, d), lambda i: (0, 0))
    return pl.pallas_call(
        _ln_kernel,
        grid=(m // tr,),
        in_specs=[tile, vec, vec],
        out_specs=[tile, tile],
        out_shape=[jax.ShapeDtypeStruct((m, d), F32), jax.ShapeDtypeStruct((m, d), BF16)],
        compiler_params=_compiler_params(1),
        name="layer_norm",
    )(z, g.reshape(1, d), b.reshape(1, d))


def kernel(x, p, w_in, b_in, w_conv, b_conv, conv_ln_g, conv_ln_b, w_o_attn, w_o_conv, w_out,
           w_ple_up, w_ple_gate, b_ple_gate, ln_g, ln_b):
    batch, seq, d_model = x.shape
    depth = w_in.shape[0]
    d_attn = w_o_attn.shape[1]
    d_conv = w_o_conv.shape[1]
    assert w_in.shape[2] == 4 * d_attn + 3 * d_conv + 2 * d_model
    alpha = (2.0 * depth) ** 0.25
    m = batch * seq
    x32 = x.reshape(m, d_model)
    x16 = x32.astype(BF16)
    for i in range(depth):
        h = _in_proj(x16, w_in[i].astype(BF16), b_in[i], d_attn=d_attn, d_conv=d_conv)
        a = _attention(h, batch=batch, seq=seq, d_attn=d_attn)
        cbr = _conv_branch(h, w_conv[i], b_conv[i], conv_ln_g[i], conv_ln_b[i],
                           seq=seq, d_attn=d_attn, d_conv=d_conv)
        merged = _merge(a, cbr, h, w_o_attn[i].astype(BF16), w_o_conv[i].astype(BF16),
                        d_attn=d_attn, d_conv=d_conv)
        y32, y16 = _out_proj(merged, w_out[i].astype(BF16), x32, alpha=alpha)
        z = _ple(y32, y16, p[i].reshape(m, -1), w_ple_gate[i].astype(BF16),
                 b_ple_gate[i], w_ple_up[i].astype(BF16))
        x32, x16 = _layer_norm(z, ln_g[i], ln_b[i])
    return x32.reshape(batch, seq, d_model)
```

```python
import functools
import math

import jax
import jax.numpy as jnp
from jax import lax
from jax.experimental import pallas as pl
from jax.experimental.pallas import tpu as pltpu

HEAD_DIM = 128
SUBLANES = 8
MOBA_BLOCK = 256
MOBA_TOP_K = 3
LN_EPS = 1e-5
CONV_HALO = 32
NEG_BIG = -1e30
Q_SCALE = HEAD_DIM ** -0.5 * math.log2(math.e)
ATTN_GROUP = 4
ATTN_HEADS_PER_STEP = 2
V7X_VMEM_LIMIT_BYTES = 56 * 1024 * 1024

F32 = jnp.float32
BF16 = jnp.bfloat16


def _compiler_params(n_axes):
    return pltpu.CompilerParams(dimension_semantics=("arbitrary",) * n_axes,
                                vmem_limit_bytes=V7X_VMEM_LIMIT_BYTES)


def _tile(dim, want):
    t = min(dim, want)
    assert dim % t == 0, (dim, want)
    return t


def _sigmoid(x):
    return 1.0 / (1.0 + jnp.exp(-x))


def _in_proj_kernel(x_ref, w_ref, b_ref, o_ref, *, tn, d_attn, d_conv, q_scale):
    col = pl.program_id(0) * tn
    acc = jnp.dot(x_ref[...], w_ref[...], preferred_element_type=F32) + b_ref[...]
    z_a0 = 3 * d_attn
    z_c0 = 4 * d_attn + 2 * d_conv
    gates0 = z_c0 + d_conv
    is_q = col < d_attn
    is_silu = ((col >= z_a0) & (col < z_a0 + d_attn)) | ((col >= z_c0) & (col < gates0))
    is_sig = col >= gates0
    is_plain = jnp.logical_not(is_q | is_silu | is_sig)

    @pl.when(is_q)
    def _():
        o_ref[...] = (acc * q_scale).astype(o_ref.dtype)

    @pl.when(is_plain)
    def _():
        o_ref[...] = acc.astype(o_ref.dtype)

    @pl.when(is_silu)
    def _():
        o_ref[...] = (acc * _sigmoid(acc)).astype(o_ref.dtype)

    @pl.when(is_sig)
    def _():
        o_ref[...] = _sigmoid(acc).astype(o_ref.dtype)


def _in_proj(x16, w16, b, layer, *, d_attn, d_conv):
    m, k = x16.shape
    n = w16.shape[2]
    tm, tn = _tile(m, 1024), _tile(d_attn, 1024)
    assert n % tn == 0 and d_conv % tn == 0
    kern = functools.partial(_in_proj_kernel, tn=tn, d_attn=d_attn, d_conv=d_conv,
                             q_scale=Q_SCALE)
    return pl.pallas_call(
        kern,
        grid=(n // tn, m // tm),
        in_specs=[pl.BlockSpec((tm, k), lambda j, i: (i, 0)),
                  pl.BlockSpec((None, k, tn), lambda j, i: (layer, 0, j)),
                  pl.BlockSpec((None, 1, tn), lambda j, i: (layer, 0, j))],
        out_specs=pl.BlockSpec((tm, tn), lambda j, i: (i, j)),
        out_shape=jax.ShapeDtypeStruct((m, n), BF16),
        compiler_params=_compiler_params(2),
        name="in_proj",
    )(x16, w16, b)


def _attn_kernel(q_ref, k_ref, v_ref, za_ref, o_ref, kmean_ref, vt_ref, s0_ref, s1_ref, bmax0_ref,
                 bmax1_ref, p_ref, acc_ref, stat_ref, *, n_blk, blk, group, q_blocks):
    i0 = pl.program_id(2) * q_blocks
    heads = q_ref.shape[1] // HEAD_DIM
    nq = q_blocks * blk
    n_pad = kmean_ref.shape[1]
    gk = group * blk
    nt = (((1,), (1,)), ((), ()))
    buf0, buf1 = (s0_ref, bmax0_ref), (s1_ref, bmax1_ref)

    def lanes(hh):
        return slice(hh * HEAD_DIM, (hh + 1) * HEAD_DIM)

    @pl.when(i0 == 0)
    def _():
        kmean_ref[...] = jnp.zeros_like(kmean_ref)
        vt_ref[:, HEAD_DIM:, :] = jnp.ones((heads, vt_ref.shape[1] - HEAD_DIM, vt_ref.shape[2]), BF16)
        for hh in range(heads):
            for j in range(n_blk):
                rows = slice(j * blk, (j + 1) * blk)
                kb = k_ref[rows, lanes(hh)].astype(F32)
                kmean_ref[hh, j:j + 1, :] = jnp.sum(kb, axis=0, keepdims=True) * (1.0 / blk)
                vt_ref[hh, 0:HEAD_DIM, rows] = v_ref[rows, lanes(hh)].astype(F32).T.astype(BF16)

    q = [q_ref[:, lanes(hh)] for hh in range(heads)]
    q_t = [qh.astype(F32).T.astype(BF16) for qh in q]
    lane = lax.broadcasted_iota(jnp.int32, (1, nq), 1)
    own = i0 + sum((lane >= b * blk).astype(jnp.int32) for b in range(1, q_blocks))
    n_full = i0 // group

    def scores_to(buf, hh, s_g):
        s_ref, bmax_ref = buf
        s_ref[hh] = s_g
        for u in range(group):
            bmax_ref[hh, u:u + 1, :] = jnp.max(s_g[u * blk:(u + 1) * blk], axis=0, keepdims=True)

    def stage_a(g, hh, buf):
        start = pl.multiple_of(g * gk, gk)
        scores_to(buf, hh, jnp.dot(k_ref[pl.ds(start, gk), lanes(hh)], q_t[hh],
                                   preferred_element_type=F32))

    def stage_a_tail(hh, buf):
        s_ref, bmax_ref = buf
        kpos = lax.broadcasted_iota(jnp.int32, (blk, 1), 0)
        for u in range(group):
            start = pl.multiple_of((i0 + u) * blk, blk)
            s_u = jnp.dot(k_ref[pl.ds(start, blk), lanes(hh)], q_t[hh][:, u * blk:],
                          preferred_element_type=F32)
            s_u = jnp.where(kpos <= lane[:, :nq - u * blk], s_u, NEG_BIG)
            s_ref[hh, u * blk:(u + 1) * blk, u * blk:] = s_u
            bmax_ref[hh, u:u + 1, u * blk:] = jnp.max(s_u, axis=0, keepdims=True)
            if u > 0:
                bmax_ref[hh, u:u + 1, 0:u * blk] = jnp.full((1, u * blk), NEG_BIG, F32)

    def block_masks(first_blk, hh, bmax_ref, picks, *, tail):
        m_new, chosen = stat_ref[hh, 0:1, :], []
        for u in range(group):
            ch = picks[0] == first_blk + u
            for s in picks[1:]:
                ch = ch | (s == first_blk + u)
            if tail:
                ch = ch | (own == first_blk + u)
            chosen.append(ch)
            m_new = jnp.maximum(m_new, jnp.where(ch, bmax_ref[hh, u:u + 1, :], NEG_BIG))
        return chosen, m_new

    def fold(hh, m_new, pv):
        alpha = jnp.exp2(stat_ref[hh, 0:1, :] - m_new)
        acc_ref[hh] = alpha * acc_ref[hh] + pv[0:HEAD_DIM]
        stat_ref[hh, 0:1, :] = m_new
        stat_ref[hh, 1:2, :] = alpha * stat_ref[hh, 1:2, :] + pv[HEAD_DIM:HEAD_DIM + 1]

    def stage_b(first_blk, hh, buf, picks):
        s_ref, bmax_ref = buf
        start = pl.multiple_of(first_blk * blk, blk)
        chosen, m_new = block_masks(first_blk, hh, bmax_ref, picks, tail=False)
        for u in range(group):
            shift = jnp.where(chosen[u], m_new, -NEG_BIG)
            p_u = jnp.exp2(s_ref[hh, u * blk:(u + 1) * blk, :] - shift)
            p_ref[hh, u * blk:(u + 1) * blk, :] = p_u.astype(BF16)
        fold(hh, m_new, jnp.dot(vt_ref[hh, :, pl.ds(start, gk)], p_ref[hh], preferred_element_type=F32))

    def stage_b_tail(hh, buf, picks):
        s_ref, bmax_ref = buf
        start = pl.multiple_of(i0 * blk, blk)
        chosen, m_new = block_masks(i0, hh, bmax_ref, picks, tail=True)
        for u in range(group):
            shift = jnp.where(chosen[u], m_new, -NEG_BIG)[:, u * blk:]
            p_u = jnp.exp2(s_ref[hh, u * blk:(u + 1) * blk, u * blk:] - shift)
            p_ref[hh, u * blk:(u + 1) * blk, u * blk:] = p_u.astype(BF16)
        pv = [jnp.dot(vt_ref[hh, :, pl.ds(start, (b + 1) * blk)],
                      p_ref[hh, 0:(b + 1) * blk, b * blk:(b + 1) * blk], preferred_element_type=F32)
              for b in range(q_blocks)]
        fold(hh, m_new, jnp.concatenate(pv, axis=1))

    for hh in range(heads):
        stage_a(0, hh, buf0)

    sel = []
    for hh in range(heads):
        gate = lax.dot_general(kmean_ref[hh].astype(BF16), q[hh], nt, preferred_element_type=F32)
        blk_iota = lax.broadcasted_iota(jnp.int32, gate.shape, 0)
        gate = jnp.where(blk_iota < own, gate, -jnp.inf)
        picks = []
        for _ in range(MOBA_TOP_K):
            mx = jnp.max(gate, axis=0, keepdims=True)
            idx = jnp.min(jnp.where(gate == mx, blk_iota, n_pad), axis=0, keepdims=True)
            picks.append(jnp.where(mx > -jnp.inf, idx, -1))
            gate = jnp.where(blk_iota == idx, -jnp.inf, gate)
        sel.append(picks)

    acc_ref[...] = jnp.zeros_like(acc_ref)
    stat_ref[:, 0:1, :] = jnp.full((heads, 1, nq), NEG_BIG, F32)
    stat_ref[:, 1:2, :] = jnp.zeros((heads, 1, nq), F32)
    last = jnp.maximum(n_full - 1, 0)

    def steady(g, rd, wr):
        for hh in range(heads):
            stage_a(g + 1, hh, wr)
            stage_b(g * group, hh, rd, sel[hh])

    def drain(rd, wr):
        @pl.when(n_full > 0)
        def _():
            for hh in range(heads):
                stage_a_tail(hh, wr)
                stage_b(last * group, hh, rd, sel[hh])

        @pl.when(n_full == 0)
        def _():
            for hh in range(heads):
                stage_a_tail(hh, wr)

        for hh in range(heads):
            stage_b_tail(hh, wr, sel[hh])

    def body(g, carry):
        pl.when(g % 2 == 0)(functools.partial(steady, g, buf0, buf1))
        pl.when(g % 2 == 1)(functools.partial(steady, g, buf1, buf0))
        return carry

    lax.fori_loop(0, last, body, 0)
    pl.when(last % 2 == 0)(functools.partial(drain, buf0, buf1))
    pl.when(last % 2 == 1)(functools.partial(drain, buf1, buf0))

    for hh in range(heads):
        out = (acc_ref[hh] * (1.0 / stat_ref[hh, 1:2, :])).T
        o_ref[:, lanes(hh)] = (out * za_ref[:, lanes(hh)].astype(F32)).astype(o_ref.dtype)


def _attention(h, *, batch, seq, d_attn):
    m = h.shape[0]
    n_heads = d_attn // HEAD_DIM
    blk = MOBA_BLOCK
    assert seq % blk == 0
    n_blk = seq // blk
    n_pad = -(-n_blk // 8) * 8
    group = math.gcd(n_blk, ATTN_GROUP)
    q_blocks = group
    assert group <= 8
    heads = math.gcd(n_heads, ATTN_HEADS_PER_STEP)
    hw = heads * HEAD_DIM
    nq = q_blocks * blk
    n_steps = n_blk // q_blocks
    col0 = n_heads // heads
    kern = functools.partial(_attn_kernel, n_blk=n_blk, blk=blk, group=group, q_blocks=q_blocks)
    return pl.pallas_call(
        kern,
        grid=(batch, n_heads // heads, n_steps),
        in_specs=[pl.BlockSpec((nq, hw), lambda b, hp, i: (b * n_steps + i, hp)),
                  pl.BlockSpec((seq, hw), lambda b, hp, i: (b, col0 + hp)),
                  pl.BlockSpec((seq, hw), lambda b, hp, i: (b, 2 * col0 + hp)),
                  pl.BlockSpec((nq, hw), lambda b, hp, i: (b * n_steps + i, 3 * col0 + hp))],
        out_specs=pl.BlockSpec((nq, hw), lambda b, hp, i: (b * n_steps + i, hp)),
        out_shape=jax.ShapeDtypeStruct((m, d_attn), BF16),
        scratch_shapes=[pltpu.VMEM((heads, n_pad, HEAD_DIM), F32),
                        pltpu.VMEM((heads, HEAD_DIM + 16, seq), BF16),
                        pltpu.VMEM((heads, group * blk, nq), F32),
                        pltpu.VMEM((heads, group * blk, nq), F32),
                        pltpu.VMEM((heads, 8, nq), F32),
                        pltpu.VMEM((heads, 8, nq), F32),
                        pltpu.VMEM((heads, group * blk, nq), BF16),
                        pltpu.VMEM((heads, HEAD_DIM, nq), F32),
                        pltpu.VMEM((heads, 8, nq), F32)],
        compiler_params=_compiler_params(3),
        name="moba_attention",
    )(h, h, h, h)


def _conv_kernel(cv_ref, cg_ref, hv_ref, hg_ref, zc_ref, w_ref, b_ref, g_ref, beta_ref, o_ref,
                 ext_ref, shift_ref, wb_ref, conv_ref, *, tiles_per_seq, row_chunk, ch_chunk):
    tt, c = cv_ref.shape
    width = w_ref.shape[0]
    first = (pl.program_id(0) % tiles_per_seq) == 0
    halo = hv_ref[...].astype(F32) * _sigmoid(hg_ref[...].astype(F32))
    ext_ref[0:CONV_HALO, :] = jnp.where(first, 0.0, halo)
    ext_ref[CONV_HALO:, :] = cv_ref[...].astype(F32) * _sigmoid(cg_ref[...].astype(F32))
    off = CONV_HALO - (width - 1)
    n_shift_rows = shift_ref.shape[1]
    for k in range(width):
        wb_ref[k] = jnp.broadcast_to(w_ref[k:k + 1, :], (SUBLANES, c))
    for c0 in range(0, c, ch_chunk):
        cols = slice(c0, c0 + ch_chunk)
        for s in range(1, SUBLANES):
            shift_ref[s - 1] = ext_ref[s:s + n_shift_rows, cols]
        for r0 in range(0, tt, row_chunk):
            acc = jnp.zeros((row_chunk // SUBLANES, SUBLANES, ch_chunk), F32) + b_ref[:, cols]
            for k in range(width):
                s = (off + k) % SUBLANES
                base = r0 + (off + k) - s
                if s == 0:
                    window = ext_ref[base:base + row_chunk, cols]
                else:
                    window = shift_ref[s - 1, base:base + row_chunk, :]
                acc = acc + wb_ref[k, :, cols] * window.reshape(acc.shape)
            conv_ref[r0:r0 + row_chunk, cols] = acc.reshape(row_chunk, ch_chunk)
    u = conv_ref[...]
    mu = jnp.mean(u, axis=-1, keepdims=True)
    d = u - mu
    var = jnp.mean(d * d, axis=-1, keepdims=True)
    n = d * lax.rsqrt(var + LN_EPS) * g_ref[...] + beta_ref[...]
    o_ref[...] = (n * _sigmoid(n) * zc_ref[...].astype(F32)).astype(o_ref.dtype)


def _conv_branch(h, w_conv, b_conv, ln_g, ln_b, *, seq, d_attn, d_conv):
    m = h.shape[0]
    c = d_conv
    width = w_conv.shape[0]
    assert width - 1 <= CONV_HALO and (4 * d_attn) % c == 0
    tt = _tile(seq, 256)
    assert tt % CONV_HALO == 0
    cv_blk = (4 * d_attn) // c
    tiles_per_seq = seq // tt
    hpt = tt // CONV_HALO
    ch_chunk = _tile(c, 512)
    kern = functools.partial(_conv_kernel, tiles_per_seq=tiles_per_seq,
                             row_chunk=_tile(tt, 32), ch_chunk=ch_chunk)
    row = lambda a: a.reshape(1, c)
    halo_idx = lambda i: jnp.maximum(i * hpt - 1, 0)
    vec_spec = pl.BlockSpec((1, c), lambda i: (0, 0))
    return pl.pallas_call(
        kern,
        grid=(m // tt,),
        in_specs=[pl.BlockSpec((tt, c), lambda i: (i, cv_blk)),
                  pl.BlockSpec((tt, c), lambda i: (i, cv_blk + 1)),
                  pl.BlockSpec((CONV_HALO, c), lambda i: (halo_idx(i), cv_blk)),
                  pl.BlockSpec((CONV_HALO, c), lambda i: (halo_idx(i), cv_blk + 1)),
                  pl.BlockSpec((tt, c), lambda i: (i, cv_blk + 2)),
                  pl.BlockSpec((width, c), lambda i: (0, 0)),
                  vec_spec, vec_spec, vec_spec],
        out_specs=pl.BlockSpec((tt, c), lambda i: (i, 0)),
        out_shape=jax.ShapeDtypeStruct((m, c), BF16),
        scratch_shapes=[pltpu.VMEM((tt + CONV_HALO, c), F32),
                        pltpu.VMEM((SUBLANES - 1, tt + CONV_HALO - SUBLANES, ch_chunk), F32),
                        pltpu.VMEM((width, SUBLANES, c), F32),
                        pltpu.VMEM((tt, c), F32)],
        compiler_params=_compiler_params(1),
        name="conv_branch",
    )(h, h, h, h, h, w_conv, row(b_conv), row(ln_g), row(ln_b))


def _merge_kernel(a_ref, c_ref, woa_ref, woc_ref, ga_ref, gc_ref, o_ref):
    oa = jnp.dot(a_ref[...], woa_ref[...], preferred_element_type=F32)
    oc = jnp.dot(c_ref[...], woc_ref[...], preferred_element_type=F32)
    o_ref[...] = (ga_ref[...].astype(F32) * oa + gc_ref[...].astype(F32) * oc).astype(o_ref.dtype)


def _merge(a, cbr, h, woa16, woc16, layer, *, d_attn, d_conv):
    m = a.shape[0]
    d = woa16.shape[2]
    gates0 = 4 * d_attn + 3 * d_conv
    tm, tn = _tile(m, 1024), _tile(math.gcd(d, gates0), 1024)
    ga_blk, gc_blk = gates0 // tn, (gates0 + d) // tn
    return pl.pallas_call(
        _merge_kernel,
        grid=(d // tn, m // tm),
        in_specs=[pl.BlockSpec((tm, d_attn), lambda j, i: (i, 0)),
                  pl.BlockSpec((tm, d_conv), lambda j, i: (i, 0)),
                  pl.BlockSpec((None, d_attn, tn), lambda j, i: (layer, 0, j)),
                  pl.BlockSpec((None, d_conv, tn), lambda j, i: (layer, 0, j)),
                  pl.BlockSpec((tm, tn), lambda j, i: (i, ga_blk + j)),
                  pl.BlockSpec((tm, tn), lambda j, i: (i, gc_blk + j))],
        out_specs=pl.BlockSpec((tm, tn), lambda j, i: (i, j)),
        out_shape=jax.ShapeDtypeStruct((m, d), BF16),
        compiler_params=_compiler_params(2),
        name="merge",
    )(a, cbr, woa16, woc16, h, h)


def _out_proj_kernel(mg_ref, w_ref, x_ref, y32_ref, y16_ref, *, alpha):
    y = alpha * x_ref[...] + jnp.dot(mg_ref[...], w_ref[...], preferred_element_type=F32)
    y32_ref[...] = y
    y16_ref[...] = y.astype(y16_ref.dtype)


def _out_proj(merged, w16, x32, layer, *, alpha):
    m, d = x32.shape
    tm, tn = _tile(m, 1024), _tile(d, 512)
    tile = pl.BlockSpec((tm, tn), lambda j, i: (i, j))
    return pl.pallas_call(
        functools.partial(_out_proj_kernel, alpha=alpha),
        grid=(d // tn, m // tm),
        in_specs=[pl.BlockSpec((tm, d), lambda j, i: (i, 0)),
                  pl.BlockSpec((None, d, tn), lambda j, i: (layer, 0, j)),
                  tile],
        out_specs=[tile, tile],
        out_shape=[jax.ShapeDtypeStruct((m, d), F32), jax.ShapeDtypeStruct((m, d), BF16)],
        compiler_params=_compiler_params(2),
        name="out_proj",
    )(merged, w16, x32)


def _ple_kernel(y16_ref, wpg_ref, bpg_ref, p_ref, wpu_ref, y32_ref, z_ref):
    gate = jnp.dot(y16_ref[...], wpg_ref[...], preferred_element_type=F32) + bpg_ref[...]
    up = jnp.dot(p_ref[...].astype(BF16), wpu_ref[...], preferred_element_type=F32)
    z_ref[...] = y32_ref[...] + _sigmoid(gate) * up


def _ple(y32, y16, p32, wpg16, bpg, wpu16, layer):
    m, d = y32.shape
    pd = p32.shape[2]
    tm, tn = _tile(m, 1024), _tile(d, 512)
    return pl.pallas_call(
        _ple_kernel,
        grid=(d // tn, m // tm),
        in_specs=[pl.BlockSpec((tm, d), lambda j, i: (i, 0)),
                  pl.BlockSpec((None, d, tn), lambda j, i: (layer, 0, j)),
                  pl.BlockSpec((None, 1, tn), lambda j, i: (layer, 0, j)),
                  pl.BlockSpec((None, tm, pd), lambda j, i: (layer, i, 0)),
                  pl.BlockSpec((None, pd, tn), lambda j, i: (layer, 0, j)),
                  pl.BlockSpec((tm, tn), lambda j, i: (i, j))],
        out_specs=pl.BlockSpec((tm, tn), lambda j, i: (i, j)),
        out_shape=jax.ShapeDtypeStruct((m, d), F32),
        compiler_params=_compiler_params(2),
        name="ple",
    )(y16, wpg16, bpg, p32, wpu16, y32)


def _ln_kernel(z_ref, g_ref, b_ref, o32_ref, o16_ref):
    z = z_ref[...]
    mu = jnp.mean(z, axis=-1, keepdims=True)
    dz = z - mu
    var = jnp.mean(dz * dz, axis=-1, keepdims=True)
    out = dz * lax.rsqrt(var + LN_EPS) * g_ref[...] + b_ref[...]
    o32_ref[...] = out
    o16_ref[...] = out.astype(o16_ref.dtype)


def _layer_norm(z, g, b):
    m, d = z.shape
    tr = _tile(m, 256)
    tile = pl.BlockSpec((tr, d), lambda i: (i, 0))
    vec = pl.BlockSpec((1, d), lambda i: (0, 0))
    return pl.pallas_call(
        _ln_kernel,
        grid=(m // tr,),
        in_specs=[tile, vec, vec],
        out_specs=[tile, tile],
        out_shape=[jax.ShapeDtypeStruct((m, d), F32), jax.ShapeDtypeStruct((m, d), BF16)],
        compiler_params=_compiler_params(1),
        name="layer_norm",
    )(z, g.reshape(1, d), b.reshape(1, d))


def kernel(x, p, w_in, b_in, w_conv, b_conv, conv_ln_g, conv_ln_b, w_o_attn, w_o_conv, w_out,
           w_ple_up, w_ple_gate, b_ple_gate, ln_g, ln_b):
    batch, seq, d_model = x.shape
    depth = w_in.shape[0]
    d_attn = w_o_attn.shape[1]
    d_conv = w_o_conv.shape[1]
    assert w_in.shape[2] == 4 * d_attn + 3 * d_conv + 2 * d_model
    alpha = (2.0 * depth) ** 0.25
    m = batch * seq
    x32 = x.reshape(m, d_model)
    x16 = x32.astype(BF16)
    w_in16, w_oa16, w_oc16 = w_in.astype(BF16), w_o_attn.astype(BF16), w_o_conv.astype(BF16)
    w_out16, w_pg16, w_pu16 = w_out.astype(BF16), w_ple_gate.astype(BF16), w_ple_up.astype(BF16)
    b_in3 = b_in.reshape(depth, 1, -1)
    b_pg3 = b_ple_gate.reshape(depth, 1, -1)
    p3 = p.reshape(depth, m, -1)
    for i in range(depth):
        h = _in_proj(x16, w_in16, b_in3, i, d_attn=d_attn, d_conv=d_conv)
        a = _attention(h, batch=batch, seq=seq, d_attn=d_attn)
        cbr = _conv_branch(h, w_conv[i], b_conv[i], conv_ln_g[i], conv_ln_b[i],
                           seq=seq, d_attn=d_attn, d_conv=d_conv)
        merged = _merge(a, cbr, h, w_oa16, w_oc16, i, d_attn=d_attn, d_conv=d_conv)
        y32, y16 = _out_proj(merged, w_out16, x32, i, alpha=alpha)
        z = _ple(y32, y16, p3, w_pg16, b_pg3, w_pu16, i)
        x32, x16 = _layer_norm(z, ln_g[i], ln_b[i])
    return x32.reshape(batch, seq, d_model)
```

```python
import functools
import math

import jax
import jax.numpy as jnp
from jax import lax
from jax.experimental import pallas as pl
from jax.experimental.pallas import tpu as pltpu

HEAD_DIM = 128
SUBLANES = 8
MOBA_BLOCK = 256
MOBA_TOP_K = 3
LN_EPS = 1e-5
CONV_HALO = 32
NEG_BIG = -1e30
Q_SCALE = HEAD_DIM ** -0.5 * math.log2(math.e)
ATTN_GROUP = 4
ATTN_HEADS_PER_STEP = 2
V7X_VMEM_LIMIT_BYTES = 56 * 1024 * 1024

F32 = jnp.float32
BF16 = jnp.bfloat16


def _compiler_params(n_axes):
    return pltpu.CompilerParams(dimension_semantics=("arbitrary",) * n_axes,
                                vmem_limit_bytes=V7X_VMEM_LIMIT_BYTES)


def _tile(dim, want):
    t = min(dim, want)
    assert dim % t == 0, (dim, want)
    return t


def _sigmoid(x):
    return 1.0 / (1.0 + jnp.exp(-x))


def _in_proj_kernel(x_ref, w_ref, b_ref, o_ref, *, tn, d_attn, d_conv, q_scale):
    col = pl.program_id(0) * tn
    z_a0 = 3 * d_attn
    z_c0 = 4 * d_attn + 2 * d_conv
    gates0 = z_c0 + d_conv
    is_q = col < d_attn
    is_silu = ((col >= z_a0) & (col < z_a0 + d_attn)) | ((col >= z_c0) & (col < gates0))
    is_sig = col >= gates0
    a = jnp.dot(x_ref[...], w_ref[...], preferred_element_type=F32) + b_ref[...]
    s = _sigmoid(a)
    scale = jnp.where(is_silu, s, jnp.where(is_q, q_scale, 1.0))
    o_ref[...] = jnp.where(is_sig, s, a * scale).astype(o_ref.dtype)


def _in_proj(x16, w16, b, layer, *, d_attn, d_conv):
    m, k = x16.shape
    n = w16.shape[2]
    tm, tn = _tile(m, 1024), _tile(d_attn, 1024)
    assert n % tn == 0 and d_conv % tn == 0
    kern = functools.partial(_in_proj_kernel, tn=tn, d_attn=d_attn, d_conv=d_conv,
                             q_scale=Q_SCALE)
    return pl.pallas_call(
        kern,
        grid=(n // tn, m // tm),
        in_specs=[pl.BlockSpec((tm, k), lambda j, i: (i, 0)),
                  pl.BlockSpec((None, k, tn), lambda j, i: (layer, 0, j)),
                  pl.BlockSpec((None, 1, tn), lambda j, i: (layer, 0, j))],
        out_specs=pl.BlockSpec((tm, tn), lambda j, i: (i, j)),
        out_shape=jax.ShapeDtypeStruct((m, n), BF16),
        compiler_params=_compiler_params(2),
        name="in_proj",
    )(x16, w16, b)


def _attn_kernel(q_ref, k_ref, v_ref, za_ref, o_ref, kmean_ref, vt_ref, s0_ref, s1_ref, bmax0_ref,
                 bmax1_ref, p_ref, acc_ref, stat_ref, *, n_blk, blk, group, q_blocks):
    i0 = pl.program_id(2) * q_blocks
    heads = q_ref.shape[1] // HEAD_DIM
    nq = q_blocks * blk
    n_pad = kmean_ref.shape[1]
    gk = group * blk
    nt = (((1,), (1,)), ((), ()))
    buf0, buf1 = (s0_ref, bmax0_ref), (s1_ref, bmax1_ref)

    def lanes(hh):
        return slice(hh * HEAD_DIM, (hh + 1) * HEAD_DIM)

    @pl.when(i0 == 0)
    def _():
        kmean_ref[...] = jnp.zeros_like(kmean_ref)
        vt_ref[:, HEAD_DIM:, :] = jnp.ones((heads, vt_ref.shape[1] - HEAD_DIM, vt_ref.shape[2]), BF16)
        for hh in range(heads):
            for j in range(n_blk):
                rows = slice(j * blk, (j + 1) * blk)
                kb = k_ref[rows, lanes(hh)].astype(F32)
                kmean_ref[hh, j:j + 1, :] = jnp.sum(kb, axis=0, keepdims=True) * (1.0 / blk)
                vt_ref[hh, 0:HEAD_DIM, rows] = v_ref[rows, lanes(hh)].astype(F32).T.astype(BF16)

    q = [q_ref[:, lanes(hh)] for hh in range(heads)]
    q_t = [qh.astype(F32).T.astype(BF16) for qh in q]
    lane = lax.broadcasted_iota(jnp.int32, (1, nq), 1)
    own = i0 + sum((lane >= b * blk).astype(jnp.int32) for b in range(1, q_blocks))
    n_full = i0 // group

    def scores_to(buf, hh, s_g):
        s_ref, bmax_ref = buf
        s_ref[hh] = s_g
        for u in range(group):
            bmax_ref[hh, u:u + 1, :] = jnp.max(s_g[u * blk:(u + 1) * blk], axis=0, keepdims=True)

    def stage_a(g, hh, buf):
        start = pl.multiple_of(g * gk, gk)
        scores_to(buf, hh, jnp.dot(k_ref[pl.ds(start, gk), lanes(hh)], q_t[hh],
                                   preferred_element_type=F32))

    def stage_a_tail(hh, buf):
        s_ref, bmax_ref = buf
        kpos = lax.broadcasted_iota(jnp.int32, (blk, 1), 0)
        for u in range(group):
            start = pl.multiple_of((i0 + u) * blk, blk)
            s_u = jnp.dot(k_ref[pl.ds(start, blk), lanes(hh)], q_t[hh][:, u * blk:],
                          preferred_element_type=F32)
            s_u = jnp.where(kpos <= lane[:, :nq - u * blk], s_u, NEG_BIG)
            s_ref[hh, u * blk:(u + 1) * blk, u * blk:] = s_u
            bmax_ref[hh, u:u + 1, u * blk:] = jnp.max(s_u, axis=0, keepdims=True)
            if u > 0:
                bmax_ref[hh, u:u + 1, 0:u * blk] = jnp.full((1, u * blk), NEG_BIG, F32)

    def block_masks(first_blk, hh, bmax_ref, picks, *, tail):
        m_new, chosen = stat_ref[hh, 0:1, :], []
        for u in range(group):
            ch = picks[0] == first_blk + u
            for s in picks[1:]:
                ch = ch | (s == first_blk + u)
            if tail:
                ch = ch | (own == first_blk + u)
            chosen.append(ch)
            m_new = jnp.maximum(m_new, jnp.where(ch, bmax_ref[hh, u:u + 1, :], NEG_BIG))
        return chosen, m_new

    def fold(hh, m_new, pv):
        alpha = jnp.exp2(stat_ref[hh, 0:1, :] - m_new)
        acc_ref[hh] = alpha * acc_ref[hh] + pv[0:HEAD_DIM]
        stat_ref[hh, 0:1, :] = m_new
        stat_ref[hh, 1:2, :] = alpha * stat_ref[hh, 1:2, :] + pv[HEAD_DIM:HEAD_DIM + 1]

    def stage_b(first_blk, hh, buf, picks):
        s_ref, bmax_ref = buf
        start = pl.multiple_of(first_blk * blk, blk)
        chosen, m_new = block_masks(first_blk, hh, bmax_ref, picks, tail=False)
        for u in range(group):
            shift = jnp.where(chosen[u], m_new, -NEG_BIG)
            p_u = jnp.exp2(s_ref[hh, u * blk:(u + 1) * blk, :] - shift)
            p_ref[hh, u * blk:(u + 1) * blk, :] = p_u.astype(BF16)
        fold(hh, m_new, jnp.dot(vt_ref[hh, :, pl.ds(start, gk)], p_ref[hh], preferred_element_type=F32))

    def stage_b_tail(hh, buf, picks):
        s_ref, bmax_ref = buf
        start = pl.multiple_of(i0 * blk, blk)
        chosen, m_new = block_masks(i0, hh, bmax_ref, picks, tail=True)
        for u in range(group):
            shift = jnp.where(chosen[u], m_new, -NEG_BIG)[:, u * blk:]
            p_u = jnp.exp2(s_ref[hh, u * blk:(u + 1) * blk, u * blk:] - shift)
            p_ref[hh, u * blk:(u + 1) * blk, u * blk:] = p_u.astype(BF16)
        pv = [jnp.dot(vt_ref[hh, :, pl.ds(start, (b + 1) * blk)],
                      p_ref[hh, 0:(b + 1) * blk, b * blk:(b + 1) * blk], preferred_element_type=F32)
              for b in range(q_blocks)]
        fold(hh, m_new, jnp.concatenate(pv, axis=1))

    for hh in range(heads):
        stage_a(0, hh, buf0)

    sel = []
    for hh in range(heads):
        gate = lax.dot_general(kmean_ref[hh].astype(BF16), q[hh], nt, preferred_element_type=F32)
        blk_iota = lax.broadcasted_iota(jnp.int32, gate.shape, 0)
        gate = jnp.where(blk_iota < own, gate, -jnp.inf)
        picks = []
        for _ in range(MOBA_TOP_K):
            mx = jnp.max(gate, axis=0, keepdims=True)
            idx = jnp.min(jnp.where(gate == mx, blk_iota, n_pad), axis=0, keepdims=True)
            picks.append(jnp.where(mx > -jnp.inf, idx, -1))
            gate = jnp.where(blk_iota == idx, -jnp.inf, gate)
        sel.append(picks)

    acc_ref[...] = jnp.zeros_like(acc_ref)
    stat_ref[:, 0:1, :] = jnp.full((heads, 1, nq), NEG_BIG, F32)
    stat_ref[:, 1:2, :] = jnp.zeros((heads, 1, nq), F32)
    last = jnp.maximum(n_full - 1, 0)

    def steady(g, rd, wr):
        for hh in range(heads):
            stage_a(g + 1, hh, wr)
            stage_b(g * group, hh, rd, sel[hh])

    def drain(rd, wr):
        @pl.when(n_full > 0)
        def _():
            for hh in range(heads):
                stage_a_tail(hh, wr)
                stage_b(last * group, hh, rd, sel[hh])

        @pl.when(n_full == 0)
        def _():
            for hh in range(heads):
                stage_a_tail(hh, wr)

        for hh in range(heads):
            stage_b_tail(hh, wr, sel[hh])

    def body(g, carry):
        pl.when(g % 2 == 0)(functools.partial(steady, g, buf0, buf1))
        pl.when(g % 2 == 1)(functools.partial(steady, g, buf1, buf0))
        return carry

    lax.fori_loop(0, last, body, 0)
    pl.when(last % 2 == 0)(functools.partial(drain, buf0, buf1))
    pl.when(last % 2 == 1)(functools.partial(drain, buf1, buf0))

    for hh in range(heads):
        out = (acc_ref[hh] * (1.0 / stat_ref[hh, 1:2, :])).T
        o_ref[:, lanes(hh)] = (out * za_ref[:, lanes(hh)].astype(F32)).astype(o_ref.dtype)


def _attention(h, *, batch, seq, d_attn):
    m = h.shape[0]
    n_heads = d_attn // HEAD_DIM
    blk = MOBA_BLOCK
    assert seq % blk == 0
    n_blk = seq // blk
    n_pad = -(-n_blk // 8) * 8
    group = math.gcd(n_blk, ATTN_GROUP)
    q_blocks = group
    assert group <= 8
    heads = math.gcd(n_heads, ATTN_HEADS_PER_STEP)
    hw = heads * HEAD_DIM
    nq = q_blocks * blk
    n_steps = n_blk // q_blocks
    col0 = n_heads // heads
    kern = functools.partial(_attn_kernel, n_blk=n_blk, blk=blk, group=group, q_blocks=q_blocks)
    return pl.pallas_call(
        kern,
        grid=(batch, n_heads // heads, n_steps),
        in_specs=[pl.BlockSpec((nq, hw), lambda b, hp, i: (b * n_steps + i, hp)),
                  pl.BlockSpec((seq, hw), lambda b, hp, i: (b, col0 + hp)),
                  pl.BlockSpec((seq, hw), lambda b, hp, i: (b, 2 * col0 + hp)),
                  pl.BlockSpec((nq, hw), lambda b, hp, i: (b * n_steps + i, 3 * col0 + hp))],
        out_specs=pl.BlockSpec((nq, hw), lambda b, hp, i: (b * n_steps + i, hp)),
        out_shape=jax.ShapeDtypeStruct((m, d_attn), BF16),
        scratch_shapes=[pltpu.VMEM((heads, n_pad, HEAD_DIM), F32),
                        pltpu.VMEM((heads, HEAD_DIM + 16, seq), BF16),
                        pltpu.VMEM((heads, group * blk, nq), F32),
                        pltpu.VMEM((heads, group * blk, nq), F32),
                        pltpu.VMEM((heads, 8, nq), F32),
                        pltpu.VMEM((heads, 8, nq), F32),
                        pltpu.VMEM((heads, group * blk, nq), BF16),
                        pltpu.VMEM((heads, HEAD_DIM, nq), F32),
                        pltpu.VMEM((heads, 8, nq), F32)],
        compiler_params=_compiler_params(3),
        name="moba_attention",
    )(h, h, h, h)


def _conv_kernel(cv_ref, cg_ref, hv_ref, hg_ref, zc_ref, w_ref, b_ref, g_ref, beta_ref, o_ref,
                 ext_ref, shift_ref, wb_ref, conv_ref, *, tiles_per_seq, row_chunk, ch_chunk):
    tt, c = cv_ref.shape
    width = w_ref.shape[0]
    first = (pl.program_id(0) % tiles_per_seq) == 0
    halo = hv_ref[...].astype(F32) * _sigmoid(hg_ref[...].astype(F32))
    ext_ref[0:CONV_HALO, :] = jnp.where(first, 0.0, halo)
    ext_ref[CONV_HALO:, :] = cv_ref[...].astype(F32) * _sigmoid(cg_ref[...].astype(F32))
    off = CONV_HALO - (width - 1)
    n_shift_rows = shift_ref.shape[1]
    for k in range(width):
        wb_ref[k] = jnp.broadcast_to(w_ref[k:k + 1, :], (SUBLANES, c))
    for c0 in range(0, c, ch_chunk):
        cols = slice(c0, c0 + ch_chunk)
        for s in range(1, SUBLANES):
            shift_ref[s - 1] = ext_ref[s:s + n_shift_rows, cols]
        for r0 in range(0, tt, row_chunk):
            acc = jnp.zeros((row_chunk // SUBLANES, SUBLANES, ch_chunk), F32) + b_ref[:, cols]
            for k in range(width):
                s = (off + k) % SUBLANES
                base = r0 + (off + k) - s
                if s == 0:
                    window = ext_ref[base:base + row_chunk, cols]
                else:
                    window = shift_ref[s - 1, base:base + row_chunk, :]
                acc = acc + wb_ref[k, :, cols] * window.reshape(acc.shape)
            conv_ref[r0:r0 + row_chunk, cols] = acc.reshape(row_chunk, ch_chunk)
    u = conv_ref[...]
    mu = jnp.mean(u, axis=-1, keepdims=True)
    d = u - mu
    var = jnp.mean(d * d, axis=-1, keepdims=True)
    n = d * lax.rsqrt(var + LN_EPS) * g_ref[...] + beta_ref[...]
    o_ref[...] = (n * _sigmoid(n) * zc_ref[...].astype(F32)).astype(o_ref.dtype)


def _conv_branch(h, w_conv, b_conv, ln_g, ln_b, *, seq, d_attn, d_conv):
    m = h.shape[0]
    c = d_conv
    width = w_conv.shape[0]
    assert width - 1 <= CONV_HALO and (4 * d_attn) % c == 0
    tt = _tile(seq, 256)
    assert tt % CONV_HALO == 0
    cv_blk = (4 * d_attn) // c
    tiles_per_seq = seq // tt
    hpt = tt // CONV_HALO
    ch_chunk = _tile(c, 512)
    kern = functools.partial(_conv_kernel, tiles_per_seq=tiles_per_seq,
                             row_chunk=_tile(tt, 32), ch_chunk=ch_chunk)
    row = lambda a: a.reshape(1, c)
    halo_idx = lambda i: jnp.maximum(i * hpt - 1, 0)
    vec_spec = pl.BlockSpec((1, c), lambda i: (0, 0))
    return pl.pallas_call(
        kern,
        grid=(m // tt,),
        in_specs=[pl.BlockSpec((tt, c), lambda i: (i, cv_blk)),
                  pl.BlockSpec((tt, c), lambda i: (i, cv_blk + 1)),
                  pl.BlockSpec((CONV_HALO, c), lambda i: (halo_idx(i), cv_blk)),
                  pl.BlockSpec((CONV_HALO, c), lambda i: (halo_idx(i), cv_blk + 1)),
                  pl.BlockSpec((tt, c), lambda i: (i, cv_blk + 2)),
                  pl.BlockSpec((width, c), lambda i: (0, 0)),
                  vec_spec, vec_spec, vec_spec],
        out_specs=pl.BlockSpec((tt, c), lambda i: (i, 0)),
        out_shape=jax.ShapeDtypeStruct((m, c), BF16),
        scratch_shapes=[pltpu.VMEM((tt + CONV_HALO, c), F32),
                        pltpu.VMEM((SUBLANES - 1, tt + CONV_HALO - SUBLANES, ch_chunk), F32),
                        pltpu.VMEM((width, SUBLANES, c), F32),
                        pltpu.VMEM((tt, c), F32)],
        compiler_params=_compiler_params(1),
        name="conv_branch",
    )(h, h, h, h, h, w_conv, row(b_conv), row(ln_g), row(ln_b))


def _merge_kernel(a_ref, c_ref, woa_ref, woc_ref, ga_ref, gc_ref, o_ref):
    oa = jnp.dot(a_ref[...], woa_ref[...], preferred_element_type=F32)
    oc = jnp.dot(c_ref[...], woc_ref[...], preferred_element_type=F32)
    o_ref[...] = (ga_ref[...].astype(F32) * oa + gc_ref[...].astype(F32) * oc).astype(o_ref.dtype)


def _merge(a, cbr, h, woa16, woc16, layer, *, d_attn, d_conv):
    m = a.shape[0]
    d = woa16.shape[2]
    gates0 = 4 * d_attn + 3 * d_conv
    tm, tn = _tile(m, 1024), _tile(math.gcd(d, gates0), 1024)
    ga_blk, gc_blk = gates0 // tn, (gates0 + d) // tn
    return pl.pallas_call(
        _merge_kernel,
        grid=(d // tn, m // tm),
        in_specs=[pl.BlockSpec((tm, d_attn), lambda j, i: (i, 0)),
                  pl.BlockSpec((tm, d_conv), lambda j, i: (i, 0)),
                  pl.BlockSpec((None, d_attn, tn), lambda j, i: (layer, 0, j)),
                  pl.BlockSpec((None, d_conv, tn), lambda j, i: (layer, 0, j)),
                  pl.BlockSpec((tm, tn), lambda j, i: (i, ga_blk + j)),
                  pl.BlockSpec((tm, tn), lambda j, i: (i, gc_blk + j))],
        out_specs=pl.BlockSpec((tm, tn), lambda j, i: (i, j)),
        out_shape=jax.ShapeDtypeStruct((m, d), BF16),
        compiler_params=_compiler_params(2),
        name="merge",
    )(a, cbr, woa16, woc16, h, h)


def _out_proj_kernel(mg_ref, w_ref, x_ref, y32_ref, y16_ref, *, alpha):
    y = alpha * x_ref[...] + jnp.dot(mg_ref[...], w_ref[...], preferred_element_type=F32)
    y32_ref[...] = y
    y16_ref[...] = y.astype(y16_ref.dtype)


def _out_proj(merged, w16, x32, layer, *, alpha):
    m, d = x32.shape
    tm, tn = _tile(m, 1024), _tile(d, 1024)
    tile = pl.BlockSpec((tm, tn), lambda j, i: (i, j))
    return pl.pallas_call(
        functools.partial(_out_proj_kernel, alpha=alpha),
        grid=(d // tn, m // tm),
        in_specs=[pl.BlockSpec((tm, d), lambda j, i: (i, 0)),
                  pl.BlockSpec((None, d, tn), lambda j, i: (layer, 0, j), pipeline_mode=pl.Buffered(1)),
                  tile],
        out_specs=[tile, tile],
        out_shape=[jax.ShapeDtypeStruct((m, d), F32), jax.ShapeDtypeStruct((m, d), BF16)],
        compiler_params=_compiler_params(2),
        name="out_proj",
    )(merged, w16, x32)


def _ple_kernel(y16_ref, wpg_ref, bpg_ref, p_ref, wpu_ref, y32_ref, z_ref):
    gate = jnp.dot(y16_ref[...], wpg_ref[...], preferred_element_type=F32) + bpg_ref[...]
    up = jnp.dot(p_ref[...].astype(BF16), wpu_ref[...], preferred_element_type=F32)
    z_ref[...] = y32_ref[...] + _sigmoid(gate) * up


def _ple(y32, y16, p32, wpg16, bpg, wpu16, layer):
    m, d = y32.shape
    pd = p32.shape[2]
    tm, tn = _tile(m, 1024), _tile(d, 1024)
    return pl.pallas_call(
        _ple_kernel,
        grid=(d // tn, m // tm),
        in_specs=[pl.BlockSpec((tm, d), lambda j, i: (i, 0)),
                  pl.BlockSpec((None, d, tn), lambda j, i: (layer, 0, j), pipeline_mode=pl.Buffered(1)),
                  pl.BlockSpec((None, 1, tn), lambda j, i: (layer, 0, j)),
                  pl.BlockSpec((None, tm, pd), lambda j, i: (layer, i, 0)),
                  pl.BlockSpec((None, pd, tn), lambda j, i: (layer, 0, j)),
                  pl.BlockSpec((tm, tn), lambda j, i: (i, j))],
        out_specs=pl.BlockSpec((tm, tn), lambda j, i: (i, j)),
        out_shape=jax.ShapeDtypeStruct((m, d), F32),
        compiler_params=_compiler_params(2),
        name="ple",
    )(y16, wpg16, bpg, p32, wpu16, y32)


def _ln_kernel(z_ref, g_ref, b_ref, o32_ref, *maybe_o16_ref):
    z = z_ref[...]
    mu = jnp.mean(z, axis=-1, keepdims=True)
    dz = z - mu
    var = jnp.mean(dz * dz, axis=-1, keepdims=True)
    out = dz * lax.rsqrt(var + LN_EPS) * g_ref[...] + b_ref[...]
    o32_ref[...] = out
    for o16_ref in maybe_o16_ref:
        o16_ref[...] = out.astype(o16_ref.dtype)


def _layer_norm(z, g, b, *, with_bf16_copy):
    m, d = z.shape
    tr = _tile(m, 256)
    tile = pl.BlockSpec((tr, d), lambda i: (i, 0))
    vec = pl.BlockSpec((1, d), lambda i: (0, 0))
    n_out = 2 if with_bf16_copy else 1
    return pl.pallas_call(
        _ln_kernel,
        grid=(m // tr,),
        in_specs=[tile, vec, vec],
        out_specs=[tile, tile][:n_out],
        out_shape=[jax.ShapeDtypeStruct((m, d), F32), jax.ShapeDtypeStruct((m, d), BF16)][:n_out],
        compiler_params=_compiler_params(1),
        name="layer_norm",
    )(z, g.reshape(1, d), b.reshape(1, d))


def kernel(x, p, w_in, b_in, w_conv, b_conv, conv_ln_g, conv_ln_b, w_o_attn, w_o_conv, w_out,
           w_ple_up, w_ple_gate, b_ple_gate, ln_g, ln_b):
    batch, seq, d_model = x.shape
    depth = w_in.shape[0]
    d_attn = w_o_attn.shape[1]
    d_conv = w_o_conv.shape[1]
    assert w_in.shape[2] == 4 * d_attn + 3 * d_conv + 2 * d_model
    alpha = (2.0 * depth) ** 0.25
    m = batch * seq
    x32 = x.reshape(m, d_model)
    x16 = x32.astype(BF16)
    w_in16, w_oa16, w_oc16 = w_in.astype(BF16), w_o_attn.astype(BF16), w_o_conv.astype(BF16)
    w_out16, w_pg16, w_pu16 = w_out.astype(BF16), w_ple_gate.astype(BF16), w_ple_up.astype(BF16)
    b_in3 = b_in.reshape(depth, 1, -1)
    b_pg3 = b_ple_gate.reshape(depth, 1, -1)
    p3 = p.reshape(depth, m, -1)
    for i in range(depth):
        h = _in_proj(x16, w_in16, b_in3, i, d_attn=d_attn, d_conv=d_conv)
        a = _attention(h, batch=batch, seq=seq, d_attn=d_attn)
        cbr = _conv_branch(h, w_conv[i], b_conv[i], conv_ln_g[i], conv_ln_b[i],
                           seq=seq, d_attn=d_attn, d_conv=d_conv)
        merged = _merge(a, cbr, h, w_oa16, w_oc16, i, d_attn=d_attn, d_conv=d_conv)
        y32, y16 = _out_proj(merged, w_out16, x32, i, alpha=alpha)
        z = _ple(y32, y16, p3, w_pg16, b_pg3, w_pu16, i)
        x32, *x16 = _layer_norm(z, ln_g[i], ln_b[i], with_bf16_copy=i + 1 < depth)
        x16 = x16[0] if x16 else None
    return x32.reshape(batch, seq, d_model)
```

```python
import functools
import math

import jax
import jax.numpy as jnp
from jax import lax
from jax.experimental import pallas as pl
from jax.experimental.pallas import tpu as pltpu

HEAD_DIM = 128
SUBLANES = 8
MOBA_BLOCK = 256
MOBA_TOP_K = 3
LN_EPS = 1e-5
CONV_HALO = 32
NEG_BIG = -1e30
Q_SCALE = HEAD_DIM ** -0.5 * math.log2(math.e)
ATTN_GROUP = 4
ATTN_HEADS_PER_STEP = 2
V7X_VMEM_LIMIT_BYTES = 56 * 1024 * 1024

F32 = jnp.float32
BF16 = jnp.bfloat16


def _compiler_params(n_axes):
    return pltpu.CompilerParams(dimension_semantics=("arbitrary",) * n_axes,
                                vmem_limit_bytes=V7X_VMEM_LIMIT_BYTES)


def _tile(dim, want):
    t = min(dim, want)
    assert dim % t == 0, (dim, want)
    return t


def _sigmoid(x):
    return 1.0 / (1.0 + jnp.exp(-x))


def _in_proj_kernel(x_ref, w_ref, b_ref, o_ref, *, tn, d_attn, d_conv, q_scale):
    col = pl.program_id(0) * tn
    z_a0 = 3 * d_attn
    z_c0 = 4 * d_attn + 2 * d_conv
    gates0 = z_c0 + d_conv
    is_q = col < d_attn
    is_silu = ((col >= z_a0) & (col < z_a0 + d_attn)) | ((col >= z_c0) & (col < gates0))
    is_sig = col >= gates0
    a = jnp.dot(x_ref[...], w_ref[...], preferred_element_type=F32) + b_ref[...]
    s = _sigmoid(a)
    scale = jnp.where(is_silu, s, jnp.where(is_q, q_scale, 1.0))
    o_ref[...] = jnp.where(is_sig, s, a * scale).astype(o_ref.dtype)


def _in_proj(x16, w16, b, layer, *, d_attn, d_conv):
    m, k = x16.shape
    n = w16.shape[2]
    tm, tn = _tile(m, 1024), _tile(d_attn, 1024)
    assert n % tn == 0 and d_conv % tn == 0
    kern = functools.partial(_in_proj_kernel, tn=tn, d_attn=d_attn, d_conv=d_conv,
                             q_scale=Q_SCALE)
    return pl.pallas_call(
        kern,
        grid=(n // tn, m // tm),
        in_specs=[pl.BlockSpec((tm, k), lambda j, i: (i, 0)),
                  pl.BlockSpec((None, k, tn), lambda j, i: (layer, 0, j)),
                  pl.BlockSpec((None, 1, tn), lambda j, i: (layer, 0, j))],
        out_specs=pl.BlockSpec((tm, tn), lambda j, i: (i, j)),
        out_shape=jax.ShapeDtypeStruct((m, n), BF16),
        compiler_params=_compiler_params(2),
        name="in_proj",
    )(x16, w16, b)


def _attn_kernel(q_ref, qn_ref, k_ref, v_ref, za_ref, o_ref, kmean_ref, vt_ref, s0_ref, s1_ref, bmax0_ref,
                 bmax1_ref, p_ref, acc_ref, stat_ref, par_ref, *, n_blk, blk, group, q_blocks):
    i0 = pl.program_id(2) * q_blocks
    heads = q_ref.shape[1] // HEAD_DIM
    nq = q_blocks * blk
    n_pad = kmean_ref.shape[1]
    gk = group * blk
    nt = (((1,), (1,)), ((), ()))
    buf0, buf1 = (s0_ref, bmax0_ref), (s1_ref, bmax1_ref)

    def lanes(hh):
        return slice(hh * HEAD_DIM, (hh + 1) * HEAD_DIM)

    @pl.when(i0 == 0)
    def _():
        kmean_ref[...] = jnp.zeros_like(kmean_ref)
        vt_ref[:, HEAD_DIM:, :] = jnp.ones((heads, vt_ref.shape[1] - HEAD_DIM, vt_ref.shape[2]), BF16)
        for hh in range(heads):
            for j in range(n_blk):
                rows = slice(j * blk, (j + 1) * blk)
                kb = k_ref[rows, lanes(hh)].astype(F32)
                kmean_ref[hh, j:j + 1, :] = jnp.sum(kb, axis=0, keepdims=True) * (1.0 / blk)
                vt_ref[hh, 0:HEAD_DIM, rows] = v_ref[rows, lanes(hh)].astype(F32).T.astype(BF16)

    q = [q_ref[:, lanes(hh)] for hh in range(heads)]
    q_t = [qh.astype(F32).T.astype(BF16) for qh in q]
    lane = lax.broadcasted_iota(jnp.int32, (1, nq), 1)
    own = i0 + sum((lane >= b * blk).astype(jnp.int32) for b in range(1, q_blocks))
    n_full = i0 // group

    def scores_to(buf, hh, s_g):
        s_ref, bmax_ref = buf
        s_ref[hh] = s_g
        for u in range(group):
            bmax_ref[hh, u:u + 1, :] = jnp.max(s_g[u * blk:(u + 1) * blk], axis=0, keepdims=True)

    def stage_a(g, hh, buf, queries_t=None):
        start = pl.multiple_of(g * gk, gk)
        rhs = q_t[hh] if queries_t is None else queries_t
        scores_to(buf, hh, jnp.dot(k_ref[pl.ds(start, gk), lanes(hh)], rhs, preferred_element_type=F32))

    def stage_a_tail(hh, buf):
        s_ref, bmax_ref = buf
        kpos = lax.broadcasted_iota(jnp.int32, (blk, 1), 0)
        for u in range(group):
            start = pl.multiple_of((i0 + u) * blk, blk)
            s_u = jnp.dot(k_ref[pl.ds(start, blk), lanes(hh)], q_t[hh][:, u * blk:],
                          preferred_element_type=F32)
            s_u = jnp.where(kpos <= lane[:, :nq - u * blk], s_u, NEG_BIG)
            s_ref[hh, u * blk:(u + 1) * blk, u * blk:] = s_u
            bmax_ref[hh, u:u + 1, u * blk:] = jnp.max(s_u, axis=0, keepdims=True)
            if u > 0:
                bmax_ref[hh, u:u + 1, 0:u * blk] = jnp.full((1, u * blk), NEG_BIG, F32)

    def block_masks(first_blk, hh, bmax_ref, picks, *, tail):
        m_new, chosen = stat_ref[hh, 0:1, :], []
        for u in range(group):
            ch = picks[0] == first_blk + u
            for s in picks[1:]:
                ch = ch | (s == first_blk + u)
            if tail:
                ch = ch | (own == first_blk + u)
            chosen.append(ch)
            m_new = jnp.maximum(m_new, jnp.where(ch, bmax_ref[hh, u:u + 1, :], NEG_BIG))
        return chosen, m_new

    def fold(hh, m_new, pv):
        alpha = jnp.exp2(stat_ref[hh, 0:1, :] - m_new)
        acc_ref[hh] = alpha * acc_ref[hh] + pv[0:HEAD_DIM]
        stat_ref[hh, 0:1, :] = m_new
        stat_ref[hh, 1:2, :] = alpha * stat_ref[hh, 1:2, :] + pv[HEAD_DIM:HEAD_DIM + 1]

    def stage_b(first_blk, hh, buf, picks):
        s_ref, bmax_ref = buf
        start = pl.multiple_of(first_blk * blk, blk)
        chosen, m_new = block_masks(first_blk, hh, bmax_ref, picks, tail=False)
        for u in range(group):
            shift = jnp.where(chosen[u], m_new, -NEG_BIG)
            p_u = jnp.exp2(s_ref[hh, u * blk:(u + 1) * blk, :] - shift)
            p_ref[hh, u * blk:(u + 1) * blk, :] = p_u.astype(BF16)
        fold(hh, m_new, jnp.dot(vt_ref[hh, :, pl.ds(start, gk)], p_ref[hh], preferred_element_type=F32))

    def stage_b_tail(hh, buf, picks):
        s_ref, bmax_ref = buf
        start = pl.multiple_of(i0 * blk, blk)
        chosen, m_new = block_masks(i0, hh, bmax_ref, picks, tail=True)
        for u in range(group):
            shift = jnp.where(chosen[u], m_new, -NEG_BIG)[:, u * blk:]
            p_u = jnp.exp2(s_ref[hh, u * blk:(u + 1) * blk, u * blk:] - shift)
            p_ref[hh, u * blk:(u + 1) * blk, u * blk:] = p_u.astype(BF16)
        pv = [jnp.dot(vt_ref[hh, :, pl.ds(start, (b + 1) * blk)],
                      p_ref[hh, 0:(b + 1) * blk, b * blk:(b + 1) * blk], preferred_element_type=F32)
              for b in range(q_blocks)]
        fold(hh, m_new, jnp.concatenate(pv, axis=1))

    @pl.when(i0 == 0)
    def _():
        par_ref[0] = 0
        for hh in range(heads):
            stage_a(0, hh, buf0)

    par = par_ref[0]

    sel = []
    for hh in range(heads):
        gate = lax.dot_general(kmean_ref[hh].astype(BF16), q[hh], nt, preferred_element_type=F32)
        blk_iota = lax.broadcasted_iota(jnp.int32, gate.shape, 0)
        gate = jnp.where(blk_iota < own, gate, -jnp.inf)
        picks = []
        for _ in range(MOBA_TOP_K):
            mx = jnp.max(gate, axis=0, keepdims=True)
            idx = jnp.min(jnp.where(gate == mx, blk_iota, n_pad), axis=0, keepdims=True)
            picks.append(jnp.where(mx > -jnp.inf, idx, -1))
            gate = jnp.where(blk_iota == idx, -jnp.inf, gate)
        sel.append(picks)

    acc_ref[...] = jnp.zeros_like(acc_ref)
    stat_ref[:, 0:1, :] = jnp.full((heads, 1, nq), NEG_BIG, F32)
    stat_ref[:, 1:2, :] = jnp.zeros((heads, 1, nq), F32)
    last = jnp.maximum(n_full - 1, 0)

    def finalize(hh):
        out = (acc_ref[hh] * (1.0 / stat_ref[hh, 1:2, :])).T
        o_ref[:, lanes(hh)] = (out * za_ref[:, lanes(hh)].astype(F32)).astype(o_ref.dtype)

    def steady(g, rd, wr):
        for hh in range(heads):
            stage_a(g + 1, hh, wr)
            stage_b(g * group, hh, rd, sel[hh])

    def drain(rd, wr):
        @pl.when(n_full > 0)
        def _():
            for hh in range(heads):
                stage_a_tail(hh, wr)
                stage_b(last * group, hh, rd, sel[hh])

        @pl.when(n_full == 0)
        def _():
            for hh in range(heads):
                stage_a_tail(hh, wr)

        for hh in range(heads):
            stage_a(0, hh, rd, qn_ref[:, lanes(hh)].astype(F32).T.astype(BF16))
            stage_b_tail(hh, wr, sel[hh])
            finalize(hh)

    def body(g, carry):
        pl.when((g + par) % 2 == 0)(functools.partial(steady, g, buf0, buf1))
        pl.when((g + par) % 2 == 1)(functools.partial(steady, g, buf1, buf0))
        return carry

    lax.fori_loop(0, last, body, 0)
    pl.when((last + par) % 2 == 0)(functools.partial(drain, buf0, buf1))
    pl.when((last + par) % 2 == 1)(functools.partial(drain, buf1, buf0))
    par_ref[0] = (last + par) % 2


def _attention(h, *, batch, seq, d_attn):
    m = h.shape[0]
    n_heads = d_attn // HEAD_DIM
    blk = MOBA_BLOCK
    assert seq % blk == 0
    n_blk = seq // blk
    n_pad = -(-n_blk // 8) * 8
    group = math.gcd(n_blk, ATTN_GROUP)
    q_blocks = group
    assert group <= 8
    heads = math.gcd(n_heads, ATTN_HEADS_PER_STEP)
    hw = heads * HEAD_DIM
    nq = q_blocks * blk
    n_steps = n_blk // q_blocks
    col0 = n_heads // heads
    kern = functools.partial(_attn_kernel, n_blk=n_blk, blk=blk, group=group, q_blocks=q_blocks)
    return pl.pallas_call(
        kern,
        grid=(batch, n_heads // heads, n_steps),
        in_specs=[pl.BlockSpec((nq, hw), lambda b, hp, i: (b * n_steps + i, hp)),
                  pl.BlockSpec((nq, hw), lambda b, hp, i: (b * n_steps + jnp.minimum(i + 1, n_steps - 1), hp)),
                  pl.BlockSpec((seq, hw), lambda b, hp, i: (b, col0 + hp)),
                  pl.BlockSpec((seq, hw), lambda b, hp, i: (b, 2 * col0 + hp)),
                  pl.BlockSpec((nq, hw), lambda b, hp, i: (b * n_steps + i, 3 * col0 + hp))],
        out_specs=pl.BlockSpec((nq, hw), lambda b, hp, i: (b * n_steps + i, hp)),
        out_shape=jax.ShapeDtypeStruct((m, d_attn), BF16),
        scratch_shapes=[pltpu.VMEM((heads, n_pad, HEAD_DIM), F32),
                        pltpu.VMEM((heads, HEAD_DIM + 16, seq), BF16),
                        pltpu.VMEM((heads, group * blk, nq), F32),
                        pltpu.VMEM((heads, group * blk, nq), F32),
                        pltpu.VMEM((heads, 8, nq), F32),
                        pltpu.VMEM((heads, 8, nq), F32),
                        pltpu.VMEM((heads, group * blk, nq), BF16),
                        pltpu.VMEM((heads, HEAD_DIM, nq), F32),
                        pltpu.VMEM((heads, 8, nq), F32),
                        pltpu.SMEM((1,), jnp.int32)],
        compiler_params=_compiler_params(3),
        name="moba_attention",
    )(h, h, h, h, h)


def _conv_kernel(cv_ref, cg_ref, hv_ref, hg_ref, zc_ref, w_ref, b_ref, g_ref, beta_ref, o_ref,
                 ext_ref, shift_ref, wb_ref, conv_ref, *, tiles_per_seq, row_chunk, ch_chunk):
    tt, c = cv_ref.shape
    width = w_ref.shape[0]
    first = (pl.program_id(0) % tiles_per_seq) == 0
    halo = hv_ref[...].astype(F32) * _sigmoid(hg_ref[...].astype(F32))
    ext_ref[0:CONV_HALO, :] = jnp.where(first, 0.0, halo)
    ext_ref[CONV_HALO:, :] = cv_ref[...].astype(F32) * _sigmoid(cg_ref[...].astype(F32))
    off = CONV_HALO - (width - 1)
    n_shift_rows = shift_ref.shape[1]
    for k in range(width):
        wb_ref[k] = jnp.broadcast_to(w_ref[k:k + 1, :], (SUBLANES, c))
    for c0 in range(0, c, ch_chunk):
        cols = slice(c0, c0 + ch_chunk)
        for s in range(1, SUBLANES):
            shift_ref[s - 1] = ext_ref[s:s + n_shift_rows, cols]
        for r0 in range(0, tt, row_chunk):
            acc = jnp.zeros((row_chunk // SUBLANES, SUBLANES, ch_chunk), F32) + b_ref[:, cols]
            for k in range(width):
                s = (off + k) % SUBLANES
                base = r0 + (off + k) - s
                if s == 0:
                    window = ext_ref[base:base + row_chunk, cols]
                else:
                    window = shift_ref[s - 1, base:base + row_chunk, :]
                acc = acc + wb_ref[k, :, cols] * window.reshape(acc.shape)
            conv_ref[r0:r0 + row_chunk, cols] = acc.reshape(row_chunk, ch_chunk)
    u = conv_ref[...]
    mu = jnp.mean(u, axis=-1, keepdims=True)
    d = u - mu
    var = jnp.mean(d * d, axis=-1, keepdims=True)
    n = d * lax.rsqrt(var + LN_EPS) * g_ref[...] + beta_ref[...]
    o_ref[...] = (n * _sigmoid(n) * zc_ref[...].astype(F32)).astype(o_ref.dtype)


def _conv_branch(h, w_conv, b_conv, ln_g, ln_b, *, seq, d_attn, d_conv):
    m = h.shape[0]
    c = d_conv
    width = w_conv.shape[0]
    assert width - 1 <= CONV_HALO and (4 * d_attn) % c == 0
    tt = _tile(seq, 256)
    assert tt % CONV_HALO == 0
    cv_blk = (4 * d_attn) // c
    tiles_per_seq = seq // tt
    hpt = tt // CONV_HALO
    ch_chunk = _tile(c, 512)
    kern = functools.partial(_conv_kernel, tiles_per_seq=tiles_per_seq,
                             row_chunk=_tile(tt, 32), ch_chunk=ch_chunk)
    row = lambda a: a.reshape(1, c)
    halo_idx = lambda i: jnp.maximum(i * hpt - 1, 0)
    vec_spec = pl.BlockSpec((1, c), lambda i: (0, 0))
    return pl.pallas_call(
        kern,
        grid=(m // tt,),
        in_specs=[pl.BlockSpec((tt, c), lambda i: (i, cv_blk)),
                  pl.BlockSpec((tt, c), lambda i: (i, cv_blk + 1)),
                  pl.BlockSpec((CONV_HALO, c), lambda i: (halo_idx(i), cv_blk)),
                  pl.BlockSpec((CONV_HALO, c), lambda i: (halo_idx(i), cv_blk + 1)),
                  pl.BlockSpec((tt, c), lambda i: (i, cv_blk + 2)),
                  pl.BlockSpec((width, c), lambda i: (0, 0)),
                  vec_spec, vec_spec, vec_spec],
        out_specs=pl.BlockSpec((tt, c), lambda i: (i, 0)),
        out_shape=jax.ShapeDtypeStruct((m, c), BF16),
        scratch_shapes=[pltpu.VMEM((tt + CONV_HALO, c), F32),
                        pltpu.VMEM((SUBLANES - 1, tt + CONV_HALO - SUBLANES, ch_chunk), F32),
                        pltpu.VMEM((width, SUBLANES, c), F32),
                        pltpu.VMEM((tt, c), F32)],
        compiler_params=_compiler_params(1),
        name="conv_branch",
    )(h, h, h, h, h, w_conv, row(b_conv), row(ln_g), row(ln_b))


def _merge_kernel(a_ref, c_ref, woa_ref, woc_ref, ga_ref, gc_ref, o_ref):
    oa = jnp.dot(a_ref[...], woa_ref[...], preferred_element_type=F32)
    oc = jnp.dot(c_ref[...], woc_ref[...], preferred_element_type=F32)
    o_ref[...] = (ga_ref[...].astype(F32) * oa + gc_ref[...].astype(F32) * oc).astype(o_ref.dtype)


def _merge(a, cbr, h, woa16, woc16, layer, *, d_attn, d_conv):
    m = a.shape[0]
    d = woa16.shape[2]
    gates0 = 4 * d_attn + 3 * d_conv
    tm, tn = _tile(m, 1024), _tile(math.gcd(d, gates0), 1024)
    ga_blk, gc_blk = gates0 // tn, (gates0 + d) // tn
    return pl.pallas_call(
        _merge_kernel,
        grid=(d // tn, m // tm),
        in_specs=[pl.BlockSpec((tm, d_attn), lambda j, i: (i, 0)),
                  pl.BlockSpec((tm, d_conv), lambda j, i: (i, 0)),
                  pl.BlockSpec((None, d_attn, tn), lambda j, i: (layer, 0, j)),
                  pl.BlockSpec((None, d_conv, tn), lambda j, i: (layer, 0, j)),
                  pl.BlockSpec((tm, tn), lambda j, i: (i, ga_blk + j)),
                  pl.BlockSpec((tm, tn), lambda j, i: (i, gc_blk + j))],
        out_specs=pl.BlockSpec((tm, tn), lambda j, i: (i, j)),
        out_shape=jax.ShapeDtypeStruct((m, d), BF16),
        compiler_params=_compiler_params(2),
        name="merge",
    )(a, cbr, woa16, woc16, h, h)


def _out_proj_kernel(mg_ref, w_ref, x_ref, y32_ref, y16_ref, *, alpha):
    y = alpha * x_ref[...] + jnp.dot(mg_ref[...], w_ref[...], preferred_element_type=F32)
    y32_ref[...] = y
    y16_ref[...] = y.astype(y16_ref.dtype)


def _out_proj(merged, w16, x32, layer, *, alpha):
    m, d = x32.shape
    tm, tn = _tile(m, 1024), _tile(d, 1024)
    tile = pl.BlockSpec((tm, tn), lambda j, i: (i, j))
    return pl.pallas_call(
        functools.partial(_out_proj_kernel, alpha=alpha),
        grid=(d // tn, m // tm),
        in_specs=[pl.BlockSpec((tm, d), lambda j, i: (i, 0)),
                  pl.BlockSpec((None, d, tn), lambda j, i: (layer, 0, j), pipeline_mode=pl.Buffered(1)),
                  tile],
        out_specs=[tile, tile],
        out_shape=[jax.ShapeDtypeStruct((m, d), F32), jax.ShapeDtypeStruct((m, d), BF16)],
        compiler_params=_compiler_params(2),
        name="out_proj",
    )(merged, w16, x32)


def _ple_kernel(y16_ref, wpg_ref, bpg_ref, p_ref, wpu_ref, y32_ref, z_ref):
    gate = jnp.dot(y16_ref[...], wpg_ref[...], preferred_element_type=F32) + bpg_ref[...]
    up = jnp.dot(p_ref[...].astype(BF16), wpu_ref[...], preferred_element_type=F32)
    z_ref[...] = y32_ref[...] + _sigmoid(gate) * up


def _ple(y32, y16, p32, wpg16, bpg, wpu16, layer):
    m, d = y32.shape
    pd = p32.shape[2]
    tm, tn = _tile(m, 1024), _tile(d, 1024)
    return pl.pallas_call(
        _ple_kernel,
        grid=(d // tn, m // tm),
        in_specs=[pl.BlockSpec((tm, d), lambda j, i: (i, 0)),
                  pl.BlockSpec((None, d, tn), lambda j, i: (layer, 0, j), pipeline_mode=pl.Buffered(1)),
                  pl.BlockSpec((None, 1, tn), lambda j, i: (layer, 0, j)),
                  pl.BlockSpec((None, tm, pd), lambda j, i: (layer, i, 0)),
                  pl.BlockSpec((None, pd, tn), lambda j, i: (layer, 0, j)),
                  pl.BlockSpec((tm, tn), lambda j, i: (i, j))],
        out_specs=pl.BlockSpec((tm, tn), lambda j, i: (i, j)),
        out_shape=jax.ShapeDtypeStruct((m, d), F32),
        compiler_params=_compiler_params(2),
        name="ple",
    )(y16, wpg16, bpg, p32, wpu16, y32)


def _ln_kernel(z_ref, g_ref, b_ref, o32_ref, *maybe_o16_ref):
    z = z_ref[...]
    mu = jnp.mean(z, axis=-1, keepdims=True)
    dz = z - mu
    var = jnp.mean(dz * dz, axis=-1, keepdims=True)
    out = dz * lax.rsqrt(var + LN_EPS) * g_ref[...] + b_ref[...]
    o32_ref[...] = out
    for o16_ref in maybe_o16_ref:
        o16_ref[...] = out.astype(o16_ref.dtype)


def _layer_norm(z, g, b, *, with_bf16_copy):
    m, d = z.shape
    tr = _tile(m, 256)
    tile = pl.BlockSpec((tr, d), lambda i: (i, 0))
    vec = pl.BlockSpec((1, d), lambda i: (0, 0))
    n_out = 2 if with_bf16_copy else 1
    return pl.pallas_call(
        _ln_kernel,
        grid=(m // tr,),
        in_specs=[tile, vec, vec],
        out_specs=[tile, tile][:n_out],
        out_shape=[jax.ShapeDtypeStruct((m, d), F32), jax.ShapeDtypeStruct((m, d), BF16)][:n_out],
        compiler_params=_compiler_params(1),
        name="layer_norm",
    )(z, g.reshape(1, d), b.reshape(1, d))


def kernel(x, p, w_in, b_in, w_conv, b_conv, conv_ln_g, conv_ln_b, w_o_attn, w_o_conv, w_out,
           w_ple_up, w_ple_gate, b_ple_gate, ln_g, ln_b):
    batch, seq, d_model = x.shape
    depth = w_in.shape[0]
    d_attn = w_o_attn.shape[1]
    d_conv = w_o_conv.shape[1]
    assert w_in.shape[2] == 4 * d_attn + 3 * d_conv + 2 * d_model
    alpha = (2.0 * depth) ** 0.25
    m = batch * seq
    x32 = x.reshape(m, d_model)
    x16 = x32.astype(BF16)
    w_in16, w_oa16, w_oc16 = w_in.astype(BF16), w_o_attn.astype(BF16), w_o_conv.astype(BF16)
    w_out16, w_pg16, w_pu16 = w_out.astype(BF16), w_ple_gate.astype(BF16), w_ple_up.astype(BF16)
    b_in3 = b_in.reshape(depth, 1, -1)
    b_pg3 = b_ple_gate.reshape(depth, 1, -1)
    p3 = p.reshape(depth, m, -1)
    for i in range(depth):
        h = _in_proj(x16, w_in16, b_in3, i, d_attn=d_attn, d_conv=d_conv)
        a = _attention(h, batch=batch, seq=seq, d_attn=d_attn)
        cbr = _conv_branch(h, w_conv[i], b_conv[i], conv_ln_g[i], conv_ln_b[i],
                           seq=seq, d_attn=d_attn, d_conv=d_conv)
        merged = _merge(a, cbr, h, w_oa16, w_oc16, i, d_attn=d_attn, d_conv=d_conv)
        y32, y16 = _out_proj(merged, w_out16, x32, i, alpha=alpha)
        z = _ple(y32, y16, p3, w_pg16, b_pg3, w_pu16, i)
        x32, *x16 = _layer_norm(z, ln_g[i], ln_b[i], with_bf16_copy=i + 1 < depth)
        x16 = x16[0] if x16 else None
    return x32.reshape(batch, seq, d_model)
```
